```python
import jax, jax.numpy as jnp
from jax import lax
import numpy as np

D_MODEL = 1024
BATCH = 8
SEQ = 2048
DEPTH = 4
DEC_BATCH = 128
DEC_SEQ = 4
PAST_LEN = 16384
PAGE_SIZE = 128

D_A = D_MODEL
H_A = 8
G_A = D_A // H_A
CHUNK_A = 128
D_B = D_MODEL
H_B = 8
DK = D_B // H_B
DV = D_B // H_B
CONV_W = 4
CHUNK_B = 64
ALPHA_DN = (2 * DEPTH) ** 0.25
BETA_DN = (8 * DEPTH) ** -0.25
LN_EPS = 1e-5
NORM_EPS = 1e-6
SPLITS = [D_A, 2 * D_A, 3 * D_A, 3 * D_A + 3 * D_B, 3 * D_A + 4 * D_B,
          3 * D_A + 4 * D_B + H_B, 3 * D_A + 4 * D_B + 2 * H_B,
          3 * D_A + 4 * D_B + 2 * H_B + D_MODEL]
P_IN = 3 * D_A + 4 * D_B + 2 * H_B + 2 * D_MODEL

kernel_name = "hybrid_gmlp_gdn_deepnorm_adaln_step"


def _layernorm(x, g=None, b=None):
    xf = x.astype(jnp.float32)
    mu = jnp.mean(xf, axis=-1, keepdims=True)
    var = jnp.mean(jnp.square(xf - mu), axis=-1, keepdims=True)
    y = (xf - mu) * lax.rsqrt(var + LN_EPS)
    if g is not None:
        y = y * g.astype(jnp.float32) + b.astype(jnp.float32)
    return y


def _rmsnorm(x, g):
    xf = x.astype(jnp.float32)
    return xf * lax.rsqrt(jnp.mean(jnp.square(xf), -1, keepdims=True) + NORM_EPS) * g.astype(jnp.float32)


def _l2norm(x):
    xf = x.astype(jnp.float32)
    return xf * lax.rsqrt(jnp.sum(jnp.square(xf), -1, keepdims=True) + NORM_EPS)


def _chunk_gmlp(u, v, z, w_s, b_s, lnv_g, lnv_b):
    B, L, _ = v.shape
    vn = _layernorm(v, lnv_g, lnv_b)
    if L <= CHUNK_A:
        lc, n_chunk = L, 1
    else:
        lc, n_chunk = CHUNK_A, -(-L // CHUNK_A)
    lp = lc * n_chunk
    vp = jnp.pad(vn, ((0, 0), (0, lp - L), (0, 0))).reshape(B, n_chunk, lc, H_A, G_A)
    ws = jnp.tril(w_s[:, :lc, :lc].astype(jnp.float32))
    s = jnp.einsum('hts,bnshc->bnthc', ws, vp) + b_s[:, :lc].astype(jnp.float32).T[None, None, :, :, None]
    s = s.reshape(B, lp, D_A)[:, :L]
    y = u.astype(jnp.float32) * s * jax.nn.silu(z.astype(jnp.float32))
    start = ((L - 1) // CHUNK_A) * CHUNK_A
    return y, vn[:, start:]


def _short_conv(x, buf, w):
    L = x.shape[1]
    xp = jnp.concatenate([buf.astype(x.dtype), x], axis=1)
    y = w[0] * xp[:, 0:L]
    for j in range(1, CONV_W):
        y = y + w[j] * xp[:, j:j + L]
    return jax.nn.silu(y), xp[:, -(CONV_W - 1):]


def _gated_delta(q, k, v, g, beta, s0):
    f32 = jnp.float32
    q, k, v, g, beta = (t.astype(f32) for t in (q, k, v, g, beta))
    B, H, L, dk = q.shape
    dv = v.shape[-1]
    C = min(CHUNK_B, L)
    pad = (-L) % C
    if pad:
        p4 = ((0, 0), (0, 0), (0, pad), (0, 0))
        p3 = ((0, 0), (0, 0), (0, pad))
        q, k, v = jnp.pad(q, p4), jnp.pad(k, p4), jnp.pad(v, p4)
        g, beta = jnp.pad(g, p3), jnp.pad(beta, p3)
    N = (L + pad) // C
    q = q * (dk ** -0.5)
    q = q.reshape(B, H, N, C, dk)
    k = k.reshape(B, H, N, C, dk)
    v = v.reshape(B, H, N, C, dv)
    beta = beta.reshape(B, H, N, C)
    gc = jnp.cumsum(g.reshape(B, H, N, C), axis=-1)
    kb = k * beta[..., None]
    vb = v * beta[..., None]
    tril_incl = jnp.tril(jnp.ones((C, C), dtype=bool))
    strict = jnp.tril(jnp.ones((C, C), dtype=bool), -1)
    decay = jnp.exp(jnp.where(tril_incl, gc[..., :, None] - gc[..., None, :], -jnp.inf))
    a_mat = jnp.where(strict, jnp.einsum('bhnid,bhnjd->bhnij', kb, k) * decay, 0.0)
    lhs = a_mat + jnp.eye(C, dtype=f32)
    u = lax.linalg.triangular_solve(lhs, vb, left_side=True, lower=True, unit_diagonal=True)
    w = lax.linalg.triangular_solve(lhs, kb * jnp.exp(gc)[..., None], left_side=True, lower=True, unit_diagonal=True)
    qk = jnp.where(tril_incl, jnp.einsum('bhnid,bhnjd->bhnij', q, k) * decay, 0.0)

    def step(S, xs):
        q_n, k_n, u_n, w_n, qk_n, gc_n = xs
        v_new = u_n - jnp.einsum('bhck,bhkv->bhcv', w_n, S)
        o = (jnp.einsum('bhck,bhkv->bhcv', q_n * jnp.exp(gc_n)[..., None], S)
             + jnp.einsum('bhcs,bhsv->bhcv', qk_n, v_new))
        g_last = gc_n[..., -1:]
        S = (S * jnp.exp(g_last)[..., None]
             + jnp.einsum('bhck,bhcv->bhkv', k_n * jnp.exp(g_last - gc_n)[..., None], v_new))
        return S, o

    xs = tuple(jnp.moveaxis(t, 2, 0) for t in (q, k, u, w, qk, gc))
    s_fin, o = lax.scan(step, s0.astype(f32), xs)
    o = jnp.moveaxis(o, 0, 2).reshape(B, H, N * C, dv)[:, :, :L]
    return o, s_fin


def _layer(x, c, conv_buf, s0, w_ada, b_ada, w_in, w_s, b_s, lnv_g, lnv_b, conv_w,
           a_log, dt_bias, onorm_g, w_pa, w_pb, w_o, ln_g, ln_b):
    B, L, _ = x.shape
    mod = jax.nn.silu(c) @ w_ada + b_ada
    shift, scale, gate = jnp.split(mod, 3, axis=-1)
    h = _layernorm(x) * (1.0 + scale[:, None].astype(jnp.float32)) + shift[:, None].astype(jnp.float32)
    h = h.astype(x.dtype)
    p = h @ w_in
    u_a, v_a, z_a, qkv, z_b, b_raw, a_raw, ga, gb = jnp.split(p, SPLITS, axis=-1)
    y_a, v_rows = _chunk_gmlp(jax.nn.gelu(u_a), jax.nn.gelu(v_a), z_a, w_s, b_s, lnv_g, lnv_b)
    qkv_c, conv_new = _short_conv(qkv, conv_buf, conv_w)
    q, k, v = jnp.split(qkv_c, 3, axis=-1)
    q = _l2norm(q.reshape(B, L, H_B, DK))
    k = _l2norm(k.reshape(B, L, H_B, DK))
    v = v.reshape(B, L, H_B, DV)
    g = -jnp.exp(a_log.astype(jnp.float32)) * jax.nn.softplus(a_raw.astype(jnp.float32) + dt_bias.astype(jnp.float32))
    beta = jax.nn.sigmoid(b_raw.astype(jnp.float32))
    o, s_new = _gated_delta(q.transpose(0, 2, 1, 3), k.transpose(0, 2, 1, 3), v.transpose(0, 2, 1, 3),
                            g.transpose(0, 2, 1), beta.transpose(0, 2, 1), s0)
    o = _rmsnorm(o.transpose(0, 2, 1, 3), onorm_g).reshape(B, L, D_B)
    y_b = o * jax.nn.silu(z_b.astype(jnp.float32))
    m = (jax.nn.sigmoid(ga.astype(jnp.float32)) * (y_a.astype(x.dtype) @ w_pa)
         + jax.nn.sigmoid(gb.astype(jnp.float32)) * (y_b.astype(x.dtype) @ w_pb))
    out = m.astype(x.dtype) @ w_o
    x_new = _layernorm(ALPHA_DN * x.astype(jnp.float32) + gate[:, None].astype(jnp.float32) * out.astype(jnp.float32),
                       ln_g, ln_b).astype(x.dtype)
    return x_new, conv_new, s_new, v_rows


def setup_inputs(seed: int = 0) -> dict:
    key = jax.random.key(seed)
    ks = jax.random.split(key, 24)
    f32 = jnp.float32

    def nrm(k, shape, s):
        return jax.random.normal(k, shape, f32) * s

    return {
        "x_prompt": nrm(ks[0], (BATCH, SEQ, D_MODEL), 1.0),
        "x_sample": nrm(ks[1], (DEC_BATCH, DEC_SEQ, D_MODEL), 1.0),
        "state_conv": nrm(ks[2], (DEPTH, DEC_BATCH, CONV_W - 1, 3 * D_B), 1.0),
        "state_ssm": nrm(ks[3], (DEPTH, DEC_BATCH, H_B, DK, DV), DK ** -0.5),
        "c_prompt": nrm(ks[4], (BATCH, D_MODEL), 1.0),
        "c_sample": nrm(ks[5], (DEC_BATCH, D_MODEL), 1.0),
        "w_ada": nrm(ks[6], (DEPTH, D_MODEL, 3 * D_MODEL), 0.5 * D_MODEL ** -0.5),
        "b_ada": nrm(ks[7], (DEPTH, 3 * D_MODEL), 0.02),
        "w_in": nrm(ks[8], (DEPTH, D_MODEL, P_IN), D_MODEL ** -0.5),
        "w_s": nrm(ks[9], (DEPTH, H_A, CHUNK_A, CHUNK_A), CHUNK_A ** -0.5),
        "b_s": nrm(ks[10], (DEPTH, H_A, CHUNK_A), 0.02),
        "lnv_g": 1.0 + nrm(ks[11], (DEPTH, D_A), 0.02),
        "lnv_b": nrm(ks[12], (DEPTH, D_A), 0.02),
        "conv_w": nrm(ks[13], (DEPTH, CONV_W, 3 * D_B), CONV_W ** -0.5),
        "a_log": jnp.log(jax.random.uniform(ks[14], (DEPTH, H_B), f32, 1.0, 16.0)),
        "dt_bias": nrm(ks[15], (DEPTH, H_B), 0.1),
        "onorm_g": 1.0 + nrm(ks[16], (DEPTH, DV), 0.02),
        "w_pa": nrm(ks[17], (DEPTH, D_A, D_MODEL), BETA_DN * D_A ** -0.5),
        "w_pb": nrm(ks[18], (DEPTH, D_B, D_MODEL), BETA_DN * D_B ** -0.5),
        "w_o": nrm(ks[19], (DEPTH, D_MODEL, D_MODEL), BETA_DN * D_MODEL ** -0.5),
        "ln_g": 1.0 + nrm(ks[20], (DEPTH, D_MODEL), 0.02),
        "ln_b": nrm(ks[21], (DEPTH, D_MODEL), 0.02),
    }


def reference(x_prompt, x_sample, state_conv, state_ssm, c_prompt, c_sample, w_ada, b_ada, w_in,
              w_s, b_s, lnv_g, lnv_b, conv_w, a_log, dt_bias, onorm_g, w_pa, w_pb, w_o, ln_g, ln_b):
    xp, xs = x_prompt, x_sample
    bp = x_prompt.shape[0]
    conv_p, ssm_p, vrow_p, conv_s, ssm_s, vrow_s = [], [], [], [], [], []
    for l in range(DEPTH):
        wl = (w_ada[l], b_ada[l], w_in[l], w_s[l], b_s[l], lnv_g[l], lnv_b[l], conv_w[l],
              a_log[l], dt_bias[l], onorm_g[l], w_pa[l], w_pb[l], w_o[l], ln_g[l], ln_b[l])
        buf0 = jnp.zeros((bp, CONV_W - 1, 3 * D_B), x_prompt.dtype)
        s_zero = jnp.zeros((bp, H_B, DK, DV), jnp.float32)
        xp, cp, sp, vp = _layer(xp, c_prompt, buf0, s_zero, *wl)
        xs, cs, ss, vs = _layer(xs, c_sample, state_conv[l], state_ssm[l], *wl)
        conv_p.append(cp.astype(state_conv.dtype))
        ssm_p.append(sp.astype(state_ssm.dtype))
        vrow_p.append(vp.astype(x_prompt.dtype))
        conv_s.append(cs.astype(state_conv.dtype))
        ssm_s.append(ss.astype(state_ssm.dtype))
        vrow_s.append(vs.astype(x_sample.dtype))
    return (xp, xs, jnp.stack(conv_p), jnp.stack(ssm_p), jnp.stack(vrow_p),
            jnp.stack(conv_s), jnp.stack(ssm_s), jnp.stack(vrow_s))
```

```python
import functools

import jax
import jax.numpy as jnp
from jax import lax
from jax.experimental import pallas as pl
from jax.experimental.pallas import tpu as pltpu

D_MODEL = 1024
N_HEAD = 8
HEAD_DIM = 128
CONV_W = 4
GMLP_CHUNK = 128
DELTA_CHUNK = 128
NEUMANN_STEPS = DELTA_CHUNK.bit_length() - 2
LN_EPS = 1e-5
NORM_EPS = 1e-6
PROMPT_TILE = 256
SAMPLE_GROUP = 8
VMEM_LIMIT_BYTES = 56 * 1024 * 1024

F32 = jnp.float32
BF16 = jnp.bfloat16


def _dot(a, b):
    return jnp.dot(a, b, preferred_element_type=F32)


def _dot_nt(a, b):
    return lax.dot_general(a, b, (((1,), (1,)), ((), ())), preferred_element_type=F32)


def _dot_tn(a, b):
    return lax.dot_general(a, b, (((0,), (0,)), ((), ())), preferred_element_type=F32)


def _normalize(x):
    mu = jnp.mean(x, axis=-1, keepdims=True)
    xc = x - mu
    var = jnp.mean(xc * xc, axis=-1, keepdims=True)
    return xc * lax.rsqrt(var + LN_EPS)


def _gelu_tanh(x):
    return x * (0.5 * (1.0 + jnp.tanh(0.7978845608028654 * (x + 0.044715 * (x * x * x)))))


def _sigmoid(x):
    return 1.0 / (1.0 + jnp.exp(-x))


def _silu(x):
    return x * _sigmoid(x)


def _softplus(x):
    return jnp.maximum(x, 0.0) + jnp.log1p(jnp.exp(-jnp.abs(x)))


def _head_cols(h):
    return slice(h * HEAD_DIM, (h + 1) * HEAD_DIM)


def _l2norm_heads(y, dst_ref, scale=None):
    for h in range(N_HEAD):
        blk = y[:, _head_cols(h)]
        ss = jnp.sum(blk * blk, axis=-1, keepdims=True)
        out = blk * lax.rsqrt(ss + NORM_EPS)
        if scale is not None:
            out = out * scale
        dst_ref[:, _head_cols(h)] = out


def _rmsnorm_heads(o, gain_row):
    parts = []
    for h in range(N_HEAD):
        blk = o[:, _head_cols(h)]
        ms = jnp.mean(blk * blk, axis=-1, keepdims=True)
        parts.append(blk * lax.rsqrt(ms + NORM_EPS))
    return jnp.concatenate(parts, axis=1) * gain_row


def _split3_bf16(x):
    x1 = x.astype(BF16)
    r1 = x - x1.astype(F32)
    x2 = r1.astype(BF16)
    r2 = r1 - x2.astype(F32)
    return x1, x2, r2.astype(BF16)


def _ada_kernel(c_ref, w_ref, b_ref, o_ref):
    a = _silu(c_ref[...]).astype(BF16)
    o_ref[0] = _dot(a, w_ref[0].astype(BF16)) + b_ref[0]


def _ada_mod(c_all, w_ada, b_ada):
    depth, d, d3 = w_ada.shape
    n = c_all.shape[0]
    ncol = d3 // d
    return pl.pallas_call(
        _ada_kernel,
        grid=(depth, ncol),
        in_specs=[
            pl.BlockSpec((n, d), lambda l, j: (0, 0)),
            pl.BlockSpec((1, d, d), lambda l, j: (l, 0, j)),
            pl.BlockSpec((1, 1, d), lambda l, j: (l, 0, j)),
        ],
        out_specs=pl.BlockSpec((1, n, d), lambda l, j: (l, 0, j)),
        out_shape=jax.ShapeDtypeStruct((depth, n, d3), F32),
        compiler_params=pltpu.CompilerParams(
            dimension_semantics=("arbitrary", "arbitrary"),
            vmem_limit_bytes=VMEM_LIMIT_BYTES),
        name="ada_mod",
    )(c_all, w_ada, b_ada.reshape(depth, 1, d3))


def _prompt_kernel(alpha,
                   x_ref, mod_ref, wuvz_ref, wqkv_ref, wab_ref, wzb_ref, wg_ref, wpa_ref, wpb_ref,
                   wo_ref, ws_ref, bs_ref, lnvg_ref, lnvb_ref, convw_ref, alog_ref, dtb_ref,
                   onorm_ref, lng_ref, lnb_ref,
                   y_ref, conv_ref, ssm_ref, vrow_ref,
                   h_s, ya_s, q_s, k_s, v_s, cbuf_s, carry_s, gc_s, beta_s, gct_s, o_s, state_s):
    D = D_MODEL
    T = x_ref.shape[1]
    i = pl.program_id(1)
    last = pl.num_programs(1) - 1
    n_gmlp = T // GMLP_CHUNK
    n_delta = T // DELTA_CHUNK
    C = DELTA_CHUNK

    @pl.when(i == 0)
    def _():
        state_s[...] = jnp.zeros_like(state_s)
        carry_s[...] = jnp.zeros_like(carry_s)

    x = x_ref[0]
    mod = mod_ref[0]
    shift, scale, gate = mod[:, :D], mod[:, D:2 * D], mod[:, 2 * D:]
    h_s[...] = (_normalize(x) * (1.0 + scale) + shift).astype(BF16)
    hb = h_s[...]

    vn = _normalize(_gelu_tanh(_dot(hb, wuvz_ref[:, D:2 * D]))) * lnvg_ref[...] + lnvb_ref[...]

    @pl.when(i == last)
    def _():
        vrow_ref[0] = vn[T - GMLP_CHUNK:, :]

    vnb = vn.astype(BF16)
    row = lax.broadcasted_iota(jnp.int32, (C, C), 0)
    col = lax.broadcasted_iota(jnp.int32, (C, C), 1)
    tril_incl = row >= col
    tril_strict = row > col
    for h in range(N_HEAD):
        wsh = jnp.where(tril_incl, ws_ref[h], 0.0).astype(BF16)
        for c in range(n_gmlp):
            rows = slice(c * GMLP_CHUNK, (c + 1) * GMLP_CHUNK)
            o_s[rows, _head_cols(h)] = _dot(wsh, vnb[rows, _head_cols(h)]) + bs_ref[:, _head_cols(h)]
    gu = _gelu_tanh(_dot(hb, wuvz_ref[:, :D]))
    ya = (gu * o_s[...]) * _silu(_dot(hb, wuvz_ref[:, 2 * D:]))
    ya_s[...] = ya.astype(BF16)

    for seg, dst in enumerate((q_s, k_s, v_s)):
        cols = slice(seg * D, (seg + 1) * D)
        p = _dot(hb, wqkv_ref[:, cols])
        cbuf_s[0:8, :] = carry_s[seg]
        cbuf_s[8:8 + T, :] = p
        carry_s[seg] = cbuf_s[T:T + 8, :]

        @pl.when(i == last)
        def _():
            conv_ref[0, :, cols] = cbuf_s[T + 8 - (CONV_W - 1):T + 8, :]

        cw = convw_ref[:, cols]
        y = cw[0:1, :] * cbuf_s[5:5 + T, :]
        y = y + cw[1:2, :] * cbuf_s[6:6 + T, :]
        y = y + cw[2:3, :] * cbuf_s[7:7 + T, :]
        y = y + cw[3:4, :] * p
        y = _silu(y)
        if seg == 0:
            _l2norm_heads(y, dst, scale=HEAD_DIM ** -0.5)
        elif seg == 1:
            _l2norm_heads(y, dst)
        else:
            dst[...] = y

    pab = _dot(hb, wab_ref[...])
    beta_s[...] = _sigmoid(pab)
    g = -jnp.exp(alog_ref[...]) * _softplus(pab + dtb_ref[...])
    rt = lax.broadcasted_iota(jnp.int32, (T, T), 0)
    ct = lax.broadcasted_iota(jnp.int32, (T, T), 1)
    ltri = jnp.where((rt // C == ct // C) & (ct <= rt), 1.0, 0.0).astype(BF16)
    g1, g2, g3 = _split3_bf16(g)
    gc = _dot(ltri, g1) + _dot(ltri, g2) + _dot(ltri, g3)
    gc_s[...] = gc
    for c in range(n_delta):
        gct_s[c] = gc[c * C:(c + 1) * C, :].T

    eye = jnp.where(row == col, 1.0, 0.0)

    def chunk_body(c, carry):
        r0 = pl.multiple_of(c * C, C)
        rows = pl.ds(r0, C)
        gc_blk = gc_s[rows, :]
        beta_blk = beta_s[rows, :]
        gct_blk = gct_s[c]
        for h in range(N_HEAD):
            cols = _head_cols(h)
            q = q_s[rows, cols]
            k = k_s[rows, cols]
            v = v_s[rows, cols]
            gcol = jnp.broadcast_to(gc_blk[:, N_HEAD + h:N_HEAD + h + 1], (C, C))
            bcol = jnp.broadcast_to(beta_blk[:, h:h + 1], (C, C))
            grow = gct_blk[N_HEAD + h:N_HEAD + h + 1, :]
            glast = gcol[C - 1:C, :]
            k16 = k.astype(BF16)
            gram = _dot_nt(jnp.concatenate([k16, q.astype(BF16)], axis=0), k16)
            decay = jnp.where(tril_incl, jnp.exp(gcol - grow), 0.0)
            a_mat = jnp.where(tril_strict, gram[:C] * bcol * decay, 0.0)
            ymat = -a_mat
            pw = ymat.astype(BF16)
            for _ in range(NEUMANN_STEPS):
                pw32 = _dot(pw, pw)
                pw = pw32.astype(BF16)
                ymat = ymat + pw32 + _dot(ymat.astype(BF16), pw)
            egc = jnp.exp(gcol)
            vb = v * bcol
            kbg = (k * bcol) * egc
            rhs = jnp.concatenate([vb, kbg], axis=1)
            uw = rhs + _dot(ymat.astype(BF16), rhs.astype(BF16))
            u, w = uw[:, :C], uw[:, C:]
            s_old = state_s[h]
            wq = jnp.concatenate([w.astype(BF16), (q * egc).astype(BF16)], axis=0)
            ws_prod = _dot(wq, s_old.astype(BF16))
            v_new = u - ws_prod[:C]
            v_new16 = v_new.astype(BF16)
            qk = jnp.where(tril_incl, gram[C:] * decay, 0.0)
            o_s[rows, cols] = ws_prod[C:] + _dot(qk.astype(BF16), v_new16)
            kp = k * jnp.exp(glast - gcol)
            state_s[h] = s_old * jnp.exp(glast) + _dot_tn(kp.astype(BF16), v_new16)
        return carry

    lax.fori_loop(0, n_delta, chunk_body, 0)

    @pl.when(i == last)
    def _():
        ssm_ref[0] = state_s[...]

    yb = _rmsnorm_heads(o_s[...], onorm_ref[...]) * _silu(_dot(hb, wzb_ref[...]))
    m = (_sigmoid(_dot(hb, wg_ref[:, :D])) * _dot(ya_s[...], wpa_ref[...])
         + _sigmoid(_dot(hb, wg_ref[:, D:])) * _dot(yb.astype(BF16), wpb_ref[...]))
    out = _dot(m.astype(BF16), wo_ref[...])
    y_ref[0] = _normalize(alpha * x + gate * out) * lng_ref[...] + lnb_ref[...]


def _resident_spec(shape):
    nd = len(shape)
    return pl.BlockSpec(shape, lambda *_: (0,) * nd, pipeline_mode=pl.Buffered(1))


def _prompt_layer(x, mod, wts, alpha):
    B, L, D = x.shape
    T = min(PROMPT_TILE, L)
    assert L % T == 0 and T % DELTA_CHUNK == 0 and T % GMLP_CHUNK == 0
    nt = L // T
    in_specs = [
        pl.BlockSpec((1, T, D), lambda b, i: (b, i, 0)),
        pl.BlockSpec((1, 1, 3 * D), lambda b, i: (b, 0, 0)),
    ] + [_resident_spec(w.shape) for w in wts]
    out_specs = [
        pl.BlockSpec((1, T, D), lambda b, i: (b, i, 0)),
        pl.BlockSpec((1, CONV_W - 1, 3 * D), lambda b, i: (b, 0, 0)),
        pl.BlockSpec((1, N_HEAD, HEAD_DIM, HEAD_DIM), lambda b, i: (b, 0, 0, 0)),
        pl.BlockSpec((1, GMLP_CHUNK, D), lambda b, i: (b, 0, 0)),
    ]
    out_shape = [
        jax.ShapeDtypeStruct((B, L, D), F32),
        jax.ShapeDtypeStruct((B, CONV_W - 1, 3 * D), F32),
        jax.ShapeDtypeStruct((B, N_HEAD, HEAD_DIM, HEAD_DIM), F32),
        jax.ShapeDtypeStruct((B, GMLP_CHUNK, D), F32),
    ]
    scratch = [
        pltpu.VMEM((T, D), BF16),
        pltpu.VMEM((T, D), BF16),
        pltpu.VMEM((T, D), F32),
        pltpu.VMEM((T, D), F32),
        pltpu.VMEM((T, D), F32),
        pltpu.VMEM((T + 8, D), F32),
        pltpu.VMEM((3, 8, D), F32),
        pltpu.VMEM((T, HEAD_DIM), F32),
        pltpu.VMEM((T, HEAD_DIM), F32),
        pltpu.VMEM((T // DELTA_CHUNK, HEAD_DIM, DELTA_CHUNK), F32),
        pltpu.VMEM((T, D), F32),
        pltpu.VMEM((N_HEAD, HEAD_DIM, HEAD_DIM), F32),
    ]
    return pl.pallas_call(
        functools.partial(_prompt_kernel, alpha),
        grid=(B, nt),
        in_specs=in_specs,
        out_specs=out_specs,
        out_shape=out_shape,
        scratch_shapes=scratch,
        compiler_params=pltpu.CompilerParams(
            dimension_semantics=("arbitrary", "arbitrary"),
            vmem_limit_bytes=VMEM_LIMIT_BYTES),
        name="prompt_layer",
    )(x, mod, *wts)


def _layer_weights(l, w_in, w_s, b_s, lnv_g, lnv_b, conv_w, a_log, dt_bias, onorm_g, w_pa, w_pb, w_o,
                   ln_g, ln_b):
    D = D_MODEL
    wi = w_in[l]
    ab0 = 7 * D
    wab = jnp.pad(wi[:, ab0:ab0 + 2 * N_HEAD], ((0, 0), (0, HEAD_DIM - 2 * N_HEAD)))
    lane_pad = ((0, 0), (N_HEAD, HEAD_DIM - 2 * N_HEAD))
    mats = dict(
        wuvz=wi[:, :3 * D].astype(BF16), wqkv=wi[:, 3 * D:6 * D].astype(BF16), wab=wab.astype(BF16),
        wzb=wi[:, 6 * D:7 * D].astype(BF16), wg=wi[:, ab0 + 2 * N_HEAD:].astype(BF16),
        wpa=w_pa[l].astype(BF16), wpb=w_pb[l].astype(BF16), wo=w_o[l].astype(BF16))
    small = dict(
        lnvg=lnv_g[l][None], lnvb=lnv_b[l][None], convw=conv_w[l],
        alog=jnp.pad(a_log[l][None], lane_pad), dtb=jnp.pad(dt_bias[l][None], lane_pad),
        onorm=jnp.tile(onorm_g[l], N_HEAD)[None], lng=ln_g[l][None], lnb=ln_b[l][None])
    return mats, small


def _prompt_operands(mats, small, w_s_l, b_s_l):
    bs_tile = jnp.repeat(b_s_l.T, HEAD_DIM, axis=1)
    return (mats["wuvz"], mats["wqkv"], mats["wab"], mats["wzb"], mats["wg"], mats["wpa"], mats["wpb"],
            mats["wo"], w_s_l, bs_tile, small["lnvg"], small["lnvb"], small["convw"], small["alog"],
            small["dtb"], small["onorm"], small["lng"], small["lnb"])


def _pairs(n, strict):
    return [(t, s) for t in range(n) for s in range(t if strict else t + 1)]


def _sample_front_kernel(n_step,
                         x_ref, mod_ref, cstate_ref, wuvz_ref, wqkv_ref, wab_ref, wsrow_ref, bs_ref,
                         lnvg_ref, lnvb_ref, convw_ref, alog_ref, dtb_ref,
                         h_ref, ya_ref, vrow_ref, conv_ref, u_ref, w_ref, qg_ref, kp_ref, qkd_ref, egl_ref,
                         q_s, k_s, v_s):
    D = D_MODEL
    NB = mod_ref.shape[0]
    L = n_step

    def slab(t):
        return slice(t * NB, (t + 1) * NB)

    mod = mod_ref[...]
    shift = jnp.concatenate([mod[:, :D]] * L, axis=0)
    scale = jnp.concatenate([mod[:, D:2 * D]] * L, axis=0)
    h_ref[...] = (_normalize(x_ref[...]) * (1.0 + scale) + shift).astype(BF16)
    hb = h_ref[...]

    vn = _normalize(_gelu_tanh(_dot(hb, wuvz_ref[:, D:2 * D]))) * lnvg_ref[...] + lnvb_ref[...]
    vrow_ref[...] = vn
    gate_rows = []
    for t in range(L):
        acc = wsrow_ref[t, 0:1, :] * vn[slab(0), :]
        for s in range(1, t + 1):
            acc = acc + wsrow_ref[t, s:s + 1, :] * vn[slab(s), :]
        gate_rows.append(acc + bs_ref[t:t + 1, :])
    sgate = jnp.concatenate(gate_rows, axis=0)
    gu = _gelu_tanh(_dot(hb, wuvz_ref[:, :D]))
    ya_ref[...] = ((gu * sgate) * _silu(_dot(hb, wuvz_ref[:, 2 * D:]))).astype(BF16)

    nc = CONV_W - 1
    for seg, dst in enumerate((q_s, k_s, v_s)):
        cols = slice(seg * D, (seg + 1) * D)
        p = _dot(hb, wqkv_ref[:, cols])
        window = [cstate_ref[j, :, cols] for j in range(nc)] + [p[slab(t), :] for t in range(L)]
        for j in range(nc):
            conv_ref[j, :, cols] = window[L + j]
        cw = convw_ref[:, cols]
        ys = []
        for t in range(L):
            y = cw[0:1, :] * window[t]
            for j in range(1, CONV_W):
                y = y + cw[j:j + 1, :] * window[t + j]
            ys.append(_silu(y))
        y = jnp.concatenate(ys, axis=0)
        if seg == 0:
            _l2norm_heads(y, dst, scale=HEAD_DIM ** -0.5)
        elif seg == 1:
            _l2norm_heads(y, dst)
        else:
            dst[...] = y

    pab = _dot(hb, wab_ref[...])
    beta = _sigmoid(pab)
    g = -jnp.exp(alog_ref[...]) * _softplus(pab + dtb_ref[...])
    gcs = [g[slab(0), :]]
    for t in range(1, L):
        gcs.append(gcs[-1] + g[slab(t), :])
    egl_ref[...] = jnp.exp(gcs[L - 1])

    lane = lax.broadcasted_iota(jnp.int32, (NB, HEAD_DIM), 1)
    qkd_acc = [jnp.zeros((NB, HEAD_DIM), F32) for _ in _pairs(L, False)]
    for h in range(N_HEAD):
        cols = _head_cols(h)
        q = [q_s[slab(t), cols] for t in range(L)]
        k = [k_s[slab(t), cols] for t in range(L)]
        v = [v_s[slab(t), cols] for t in range(L)]
        gcol = [jnp.broadcast_to(gcs[t][:, N_HEAD + h:N_HEAD + h + 1], (NB, HEAD_DIM)) for t in range(L)]
        bcol = [jnp.broadcast_to(beta[slab(t), h:h + 1], (NB, HEAD_DIM)) for t in range(L)]
        egc = [jnp.exp(gcol[t]) for t in range(L)]
        us, ws = [], []
        for t in range(L):
            u_t = v[t] * bcol[t]
            w_t = (k[t] * bcol[t]) * egc[t]
            for s in range(t):
                kk = jnp.sum(k[t] * k[s], axis=-1, keepdims=True)
                a_ts = (bcol[t] * kk) * jnp.exp(gcol[t] - gcol[s])
                u_t = u_t - a_ts * us[s]
                w_t = w_t - a_ts * ws[s]
            us.append(u_t)
            ws.append(w_t)
        for t in range(L):
            u_ref[slab(t), cols] = us[t]
            w_ref[slab(t), cols] = ws[t]
            qg_ref[slab(t), cols] = q[t] * egc[t]
            kp_ref[slab(t), cols] = k[t] * jnp.exp(gcol[L - 1] - gcol[t])
        for idx, (t, s) in enumerate(_pairs(L, False)):
            qk = jnp.sum(q[t] * k[s], axis=-1, keepdims=True)
            qkd = qk * jnp.exp(gcol[t] - gcol[s])
            qkd_acc[idx] = jnp.where(lane == h, qkd, qkd_acc[idx])
    for idx in range(len(qkd_acc)):
        qkd_ref[idx] = qkd_acc[idx]


def _sample_delta_kernel(n_step,
                         u_ref, w_ref, qg_ref, kp_ref, qkd_ref, egl_ref, state_ref,
                         o_ref, state_out_ref):
    L = n_step
    G = egl_ref.shape[0]
    row_lhs = lax.broadcasted_iota(jnp.int32, (2 * L * G, HEAD_DIM), 0) % G
    row_kp = lax.broadcasted_iota(jnp.int32, (L * G, HEAD_DIM), 0) % G
    pairs = _pairs(L, False)
    for h in range(N_HEAD):
        cols = _head_cols(h)
        lhs = jnp.concatenate([w_ref[t, :, cols] for t in range(L)]
                              + [qg_ref[t, :, cols] for t in range(L)], axis=0).astype(BF16)
        acc = jnp.zeros((2 * L * G, HEAD_DIM), F32)
        for b in range(G):
            res = _dot(lhs, state_ref[b, h].astype(BF16))
            acc = jnp.where(row_lhs == b, res, acc)
        v_new = [u_ref[t, :, cols] - acc[t * G:(t + 1) * G, :] for t in range(L)]
        for t in range(L):
            o_t = acc[(L + t) * G:(L + t + 1) * G, :]
            for s in range(t + 1):
                coef = jnp.broadcast_to(qkd_ref[pairs.index((t, s)), :, h:h + 1], (G, HEAD_DIM))
                o_t = o_t + coef * v_new[s]
            o_ref[t, :, cols] = o_t
        vn16 = jnp.concatenate(v_new, axis=0).astype(BF16)
        kpg = jnp.concatenate([kp_ref[t, :, cols] for t in range(L)], axis=0)
        for b in range(G):
            kpm = jnp.where(row_kp == b, kpg, 0.0).astype(BF16)
            eg = jnp.broadcast_to(egl_ref[b:b + 1, N_HEAD + h:N_HEAD + h + 1], (HEAD_DIM, HEAD_DIM))
            state_out_ref[b, h] = state_ref[b, h] * eg + _dot_tn(kpm, vn16)


def _sample_back_kernel(alpha, n_step,
                        x_ref, mod_ref, h_ref, ya_ref, o_ref, wzb_ref, wg_ref, wpa_ref, wpb_ref, wo_ref,
                        onorm_ref, lng_ref, lnb_ref, y_ref):
    D = D_MODEL
    gate = jnp.concatenate([mod_ref[:, 2 * D:]] * n_step, axis=0)
    hb = h_ref[...]
    yb = _rmsnorm_heads(o_ref[...], onorm_ref[...]) * _silu(_dot(hb, wzb_ref[...]))
    m = (_sigmoid(_dot(hb, wg_ref[:, :D])) * _dot(ya_ref[...], wpa_ref[...])
         + _sigmoid(_dot(hb, wg_ref[:, D:])) * _dot(yb.astype(BF16), wpb_ref[...]))
    out = _dot(m.astype(BF16), wo_ref[...])
    y_ref[...] = _normalize(alpha * x_ref[...] + gate * out) * lng_ref[...] + lnb_ref[...]


def _sample_layer(x, mod, cstate, state, mats, small, wsrow, bs_rows, alpha, n_step):
    R, D = x.shape
    L = n_step
    NB = R // L
    G = SAMPLE_GROUP
    assert NB % G == 0 and L >= CONV_W - 1
    npair = len(_pairs(L, False))
    params = pltpu.CompilerParams(vmem_limit_bytes=VMEM_LIMIT_BYTES)
    act = jax.ShapeDtypeStruct((R, D), F32)
    act16 = jax.ShapeDtypeStruct((R, D), BF16)
    h16, ya, vrows, conv_new, u, w, qg, kp, qkd, egl = pl.pallas_call(
        functools.partial(_sample_front_kernel, L),
        out_shape=[act16, act16, act, jax.ShapeDtypeStruct(cstate.shape, F32), act, act, act, act,
                   jax.ShapeDtypeStruct((npair, NB, HEAD_DIM), F32),
                   jax.ShapeDtypeStruct((NB, HEAD_DIM), F32)],
        scratch_shapes=[pltpu.VMEM((R, D), F32)] * 3,
        compiler_params=params,
        name="sample_front",
    )(x, mod, cstate, mats["wuvz"], mats["wqkv"], mats["wab"], wsrow, bs_rows,
      small["lnvg"], small["lnvb"], small["convw"], small["alog"], small["dtb"])

    def rows_spec():
        return pl.BlockSpec((L, G, D), lambda i: (0, i, 0))

    state_spec = pl.BlockSpec((G, N_HEAD, HEAD_DIM, HEAD_DIM), lambda i: (i, 0, 0, 0))
    as3 = lambda a: a.reshape(L, NB, D)
    o, state_new = pl.pallas_call(
        functools.partial(_sample_delta_kernel, L),
        grid=(NB // G,),
        in_specs=[rows_spec(), rows_spec(), rows_spec(), rows_spec(),
                  pl.BlockSpec((npair, G, HEAD_DIM), lambda i: (0, i, 0)),
                  pl.BlockSpec((G, HEAD_DIM), lambda i: (i, 0)),
                  state_spec],
        out_specs=[rows_spec(), state_spec],
        out_shape=[jax.ShapeDtypeStruct((L, NB, D), F32), jax.ShapeDtypeStruct(state.shape, F32)],
        compiler_params=pltpu.CompilerParams(
            dimension_semantics=("arbitrary",), vmem_limit_bytes=VMEM_LIMIT_BYTES),
        name="sample_delta",
    )(as3(u), as3(w), as3(qg), as3(kp), qkd, egl, state)
    y = pl.pallas_call(
        functools.partial(_sample_back_kernel, alpha, L),
        out_shape=act,
        compiler_params=params,
        name="sample_back",
    )(x, mod, h16, ya, o.reshape(R, D), mats["wzb"], mats["wg"], mats["wpa"], mats["wpb"], mats["wo"],
      small["onorm"], small["lng"], small["lnb"])
    return y, conv_new, state_new, vrows


def kernel(x_prompt, x_sample, state_conv, state_ssm, c_prompt, c_sample, w_ada, b_ada, w_in, w_s, b_s,
           lnv_g, lnv_b, conv_w, a_log, dt_bias, onorm_g, w_pa, w_pb, w_o, ln_g, ln_b):
    depth = w_in.shape[0]
    alpha = (2 * depth) ** 0.25
    bp = x_prompt.shape[0]
    nb, ls, d = x_sample.shape
    assert ls <= GMLP_CHUNK
    mod = _ada_mod(jnp.concatenate([c_prompt, c_sample], axis=0), w_ada, b_ada)
    xp = x_prompt
    xs = x_sample.transpose(1, 0, 2).reshape(ls * nb, d)
    outs = [[] for _ in range(6)]
    for l in range(depth):
        mats, small = _layer_weights(l, w_in, w_s, b_s, lnv_g, lnv_b, conv_w, a_log, dt_bias, onorm_g,
                                     w_pa, w_pb, w_o, ln_g, ln_b)
        xp, cp, sp, vp = _prompt_layer(xp, mod[l, :bp, None, :], _prompt_operands(mats, small, w_s[l], b_s[l]),
                                       alpha)
        wsrow = jnp.repeat(w_s[l][:, :ls, :ls].transpose(1, 2, 0), HEAD_DIM, axis=-1)
        bs_rows = jnp.repeat(b_s[l][:, :ls].T, HEAD_DIM, axis=-1)
        xs, cs, ss, vs = _sample_layer(xs, mod[l, bp:], state_conv[l].transpose(1, 0, 2), state_ssm[l],
                                       mats, small, wsrow, bs_rows, alpha, ls)
        for lst, val in zip(outs, (cp, sp, vp, cs.transpose(1, 0, 2), ss,
                                   vs.reshape(ls, nb, d).transpose(1, 0, 2))):
            lst.append(val)
    y_sample = xs.reshape(ls, nb, d).transpose(1, 0, 2)
    return (xp, y_sample) + tuple(jnp.stack(o) for o in outs)
```

```python
import functools

import jax
import jax.numpy as jnp
from jax import lax
from jax.experimental import pallas as pl
from jax.experimental.pallas import tpu as pltpu

D_MODEL = 1024
N_HEAD = 8
HEAD_DIM = 128
CONV_W = 4
GMLP_CHUNK = 128
DELTA_CHUNK = 128
NEUMANN_STEPS = DELTA_CHUNK.bit_length() - 2
LN_EPS = 1e-5
NORM_EPS = 1e-6
PROMPT_TILE = 256
SAMPLE_GROUP = 8
VMEM_LIMIT_BYTES = 56 * 1024 * 1024

F32 = jnp.float32
BF16 = jnp.bfloat16


def _dot(a, b):
    return jnp.dot(a, b, preferred_element_type=F32)


def _dot_nt(a, b):
    return lax.dot_general(a, b, (((1,), (1,)), ((), ())), preferred_element_type=F32)


def _dot_tn(a, b):
    return lax.dot_general(a, b, (((0,), (0,)), ((), ())), preferred_element_type=F32)


def _normalize(x):
    mu = jnp.mean(x, axis=-1, keepdims=True)
    xc = x - mu
    var = jnp.mean(xc * xc, axis=-1, keepdims=True)
    return xc * lax.rsqrt(var + LN_EPS)


def _gelu_tanh(x):
    return x * (0.5 * (1.0 + jnp.tanh(0.7978845608028654 * (x + 0.044715 * (x * x * x)))))


def _sigmoid(x):
    return 1.0 / (1.0 + jnp.exp(-x))


def _silu(x):
    return x * _sigmoid(x)


def _softplus(x):
    return jnp.maximum(x, 0.0) + jnp.log1p(jnp.exp(-jnp.abs(x)))


def _head_cols(h):
    return slice(h * HEAD_DIM, (h + 1) * HEAD_DIM)


def _l2norm_heads(y, dst_ref, scale=None):
    for h in range(N_HEAD):
        blk = y[:, _head_cols(h)]
        ss = jnp.sum(blk * blk, axis=-1, keepdims=True)
        out = blk * lax.rsqrt(ss + NORM_EPS)
        if scale is not None:
            out = out * scale
        dst_ref[:, _head_cols(h)] = out


def _rmsnorm_heads(o, gain_row):
    parts = []
    for h in range(N_HEAD):
        blk = o[:, _head_cols(h)]
        ms = jnp.mean(blk * blk, axis=-1, keepdims=True)
        parts.append(blk * lax.rsqrt(ms + NORM_EPS))
    return jnp.concatenate(parts, axis=1) * gain_row


def _split3_bf16(x):
    x1 = x.astype(BF16)
    r1 = x - x1.astype(F32)
    x2 = r1.astype(BF16)
    r2 = r1 - x2.astype(F32)
    return x1, x2, r2.astype(BF16)


def _ada_kernel(c_ref, w_ref, b_ref, o_ref):
    a = _silu(c_ref[...]).astype(BF16)
    o_ref[0] = _dot(a, w_ref[0].astype(BF16)) + b_ref[0]


def _ada_mod(c_all, w_ada, b_ada):
    depth, d, d3 = w_ada.shape
    n = c_all.shape[0]
    ncol = d3 // d
    return pl.pallas_call(
        _ada_kernel,
        grid=(depth, ncol),
        in_specs=[
            pl.BlockSpec((n, d), lambda l, j: (0, 0)),
            pl.BlockSpec((1, d, d), lambda l, j: (l, 0, j)),
            pl.BlockSpec((1, 1, d), lambda l, j: (l, 0, j)),
        ],
        out_specs=pl.BlockSpec((1, n, d), lambda l, j: (l, 0, j)),
        out_shape=jax.ShapeDtypeStruct((depth, n, d3), F32),
        compiler_params=pltpu.CompilerParams(
            dimension_semantics=("arbitrary", "arbitrary"),
            vmem_limit_bytes=VMEM_LIMIT_BYTES),
        name="ada_mod",
    )(c_all, w_ada, b_ada.reshape(depth, 1, d3))


def _prompt_kernel(alpha,
                   x_ref, mod_ref, wuvz_ref, wqkv_ref, wab_ref, wzb_ref, wg_ref, wpa_ref, wpb_ref,
                   wo_ref, ws_ref, bs_ref, lnvg_ref, lnvb_ref, convw_ref, alog_ref, dtb_ref,
                   onorm_ref, lng_ref, lnb_ref,
                   y_ref, conv_ref, ssm_ref, vrow_ref,
                   h_s, ya_s, q_s, k_s, v_s, cbuf_s, carry_s, gc_s, beta_s, gct_s, o_s, state_s,
                   qk_s, wq_s, u_s, kp_s):
    D = D_MODEL
    T = x_ref.shape[1]
    i = pl.program_id(1)
    last = pl.num_programs(1) - 1
    n_gmlp = T // GMLP_CHUNK
    n_delta = T // DELTA_CHUNK
    C = DELTA_CHUNK

    @pl.when(i == 0)
    def _():
        state_s[...] = jnp.zeros_like(state_s)
        carry_s[...] = jnp.zeros_like(carry_s)

    x = x_ref[0]
    mod = mod_ref[0]
    shift, scale, gate = mod[:, :D], mod[:, D:2 * D], mod[:, 2 * D:]
    h_s[...] = (_normalize(x) * (1.0 + scale) + shift).astype(BF16)
    hb = h_s[...]

    vn = _normalize(_gelu_tanh(_dot(hb, wuvz_ref[:, D:2 * D]))) * lnvg_ref[...] + lnvb_ref[...]

    @pl.when(i == last)
    def _():
        vrow_ref[0] = vn[T - GMLP_CHUNK:, :]

    vnb = vn.astype(BF16)
    row = lax.broadcasted_iota(jnp.int32, (C, C), 0)
    col = lax.broadcasted_iota(jnp.int32, (C, C), 1)
    tril_incl = row >= col
    tril_strict = row > col
    for h in range(N_HEAD):
        wsh = jnp.where(tril_incl, ws_ref[h], 0.0).astype(BF16)
        for c in range(n_gmlp):
            rows = slice(c * GMLP_CHUNK, (c + 1) * GMLP_CHUNK)
            o_s[rows, _head_cols(h)] = _dot(wsh, vnb[rows, _head_cols(h)]) + bs_ref[:, _head_cols(h)]
    gu = _gelu_tanh(_dot(hb, wuvz_ref[:, :D]))
    ya = (gu * o_s[...]) * _silu(_dot(hb, wuvz_ref[:, 2 * D:]))
    ya_s[...] = ya.astype(BF16)

    for seg, dst in enumerate((q_s, k_s, v_s)):
        cols = slice(seg * D, (seg + 1) * D)
        p = _dot(hb, wqkv_ref[:, cols])
        cbuf_s[0:8, :] = carry_s[seg]
        cbuf_s[8:8 + T, :] = p
        carry_s[seg] = cbuf_s[T:T + 8, :]

        @pl.when(i == last)
        def _():
            conv_ref[0, :, cols] = cbuf_s[T + 8 - (CONV_W - 1):T + 8, :]

        cw = convw_ref[:, cols]
        y = cw[0:1, :] * cbuf_s[5:5 + T, :]
        y = y + cw[1:2, :] * cbuf_s[6:6 + T, :]
        y = y + cw[2:3, :] * cbuf_s[7:7 + T, :]
        y = y + cw[3:4, :] * p
        y = _silu(y)
        if seg == 0:
            _l2norm_heads(y, dst, scale=HEAD_DIM ** -0.5)
        elif seg == 1:
            _l2norm_heads(y, dst)
        else:
            dst[...] = y

    pab = _dot(hb, wab_ref[...])
    beta_s[...] = _sigmoid(pab)
    g = -jnp.exp(alog_ref[...]) * _softplus(pab + dtb_ref[...])
    rt = lax.broadcasted_iota(jnp.int32, (T, T), 0)
    ct = lax.broadcasted_iota(jnp.int32, (T, T), 1)
    ltri = jnp.where((rt // C == ct // C) & (ct <= rt), 1.0, 0.0).astype(BF16)
    g1, g2, g3 = _split3_bf16(g)
    gc = _dot(ltri, g1) + _dot(ltri, g2) + _dot(ltri, g3)
    gc_s[...] = gc
    for c in range(n_delta):
        gct_s[c] = gc[c * C:(c + 1) * C, :].T

    probs = [(c, h) for c in range(n_delta) for h in range(N_HEAD)]

    def rows_of(c):
        return slice(c * C, (c + 1) * C)

    def gcol_of(c, h):
        return jnp.broadcast_to(gc_s[rows_of(c), N_HEAD + h:N_HEAD + h + 1], (C, C))

    def bcol_of(c, h):
        return jnp.broadcast_to(beta_s[rows_of(c), h:h + 1], (C, C))

    grams = []
    for c, h in probs:
        k16 = k_s[rows_of(c), _head_cols(h)].astype(BF16)
        q16 = q_s[rows_of(c), _head_cols(h)].astype(BF16)
        grams.append(_dot_nt(jnp.concatenate([k16, q16], axis=0), k16))
    ys, pws = [], []
    for j, (c, h) in enumerate(probs):
        gcol = gcol_of(c, h)
        grow = gct_s[c, N_HEAD + h:N_HEAD + h + 1, :]
        decay = jnp.where(tril_incl, jnp.exp(gcol - grow), 0.0)
        xmat = jnp.where(tril_strict, -(grams[j][:C] * bcol_of(c, h) * decay), 0.0)
        qk_s[c, h] = jnp.where(tril_incl, grams[j][C:] * decay, 0.0).astype(BF16)
        ys.append(xmat)
        pws.append(xmat.astype(BF16))
    pw32s = [_dot(p, p) for p in pws]
    for step in range(1, NEUMANN_STEPS + 1):
        final = step == NEUMANN_STEPS
        for j in range(len(probs)):
            p16 = pw32s[j].astype(BF16)
            y16 = ys[j].astype(BF16)
            if final:
                ys[j] = ys[j] + pw32s[j] + _dot(p16, y16)
            else:
                z = _dot(p16, jnp.concatenate([p16, y16], axis=1))
                ys[j] = ys[j] + pw32s[j] + z[:, C:]
                pw32s[j] = z[:, :C]
    for j, (c, h) in enumerate(probs):
        rows, cols = rows_of(c), _head_cols(h)
        gcol, bcol = gcol_of(c, h), bcol_of(c, h)
        egc = jnp.exp(gcol)
        k = k_s[rows, cols]
        rhs = jnp.concatenate([v_s[rows, cols] * bcol, (k * bcol) * egc], axis=1)
        uw = rhs + _dot(ys[j].astype(BF16), rhs.astype(BF16))
        u_s[rows, cols] = uw[:, :C]
        wq_s[c, h, 0:C, :] = uw[:, C:].astype(BF16)
        wq_s[c, h, C:2 * C, :] = (q_s[rows, cols] * egc).astype(BF16)
        kp_s[rows, cols] = (k * jnp.exp(gcol[C - 1:C, :] - gcol)).astype(BF16)

    for c in range(n_delta):
        rows = rows_of(c)
        prods = [_dot(wq_s[c, h], state_s[h].astype(BF16)) for h in range(N_HEAD)]
        v16s = []
        for h in range(N_HEAD):
            v16s.append((u_s[rows, _head_cols(h)] - prods[h][:C]).astype(BF16))
        for h in range(N_HEAD):
            o_s[rows, _head_cols(h)] = prods[h][C:] + _dot(qk_s[c, h], v16s[h])
        for h in range(N_HEAD):
            eg = jnp.exp(jnp.broadcast_to(gc_s[c * C + C - 1:c * C + C, N_HEAD + h:N_HEAD + h + 1], (C, C)))
            state_s[h] = state_s[h] * eg + _dot_tn(kp_s[rows, _head_cols(h)], v16s[h])

    @pl.when(i == last)
    def _():
        ssm_ref[0] = state_s[...]

    yb = _rmsnorm_heads(o_s[...], onorm_ref[...]) * _silu(_dot(hb, wzb_ref[...]))
    m = (_sigmoid(_dot(hb, wg_ref[:, :D])) * _dot(ya_s[...], wpa_ref[...])
         + _sigmoid(_dot(hb, wg_ref[:, D:])) * _dot(yb.astype(BF16), wpb_ref[...]))
    out = _dot(m.astype(BF16), wo_ref[...])
    y_ref[0] = _normalize(alpha * x + gate * out) * lng_ref[...] + lnb_ref[...]


def _resident_spec(shape):
    nd = len(shape)
    return pl.BlockSpec(shape, lambda *_: (0,) * nd, pipeline_mode=pl.Buffered(1))


def _prompt_layer(x, mod, wts, alpha):
    B, L, D = x.shape
    T = min(PROMPT_TILE, L)
    assert L % T == 0 and T % DELTA_CHUNK == 0 and T % GMLP_CHUNK == 0
    nt = L // T
    in_specs = [
        pl.BlockSpec((1, T, D), lambda b, i: (b, i, 0)),
        pl.BlockSpec((1, 1, 3 * D), lambda b, i: (b, 0, 0)),
    ] + [_resident_spec(w.shape) for w in wts]
    out_specs = [
        pl.BlockSpec((1, T, D), lambda b, i: (b, i, 0)),
        pl.BlockSpec((1, CONV_W - 1, 3 * D), lambda b, i: (b, 0, 0)),
        pl.BlockSpec((1, N_HEAD, HEAD_DIM, HEAD_DIM), lambda b, i: (b, 0, 0, 0)),
        pl.BlockSpec((1, GMLP_CHUNK, D), lambda b, i: (b, 0, 0)),
    ]
    out_shape = [
        jax.ShapeDtypeStruct((B, L, D), F32),
        jax.ShapeDtypeStruct((B, CONV_W - 1, 3 * D), F32),
        jax.ShapeDtypeStruct((B, N_HEAD, HEAD_DIM, HEAD_DIM), F32),
        jax.ShapeDtypeStruct((B, GMLP_CHUNK, D), F32),
    ]
    scratch = [
        pltpu.VMEM((T, D), BF16),
        pltpu.VMEM((T, D), BF16),
        pltpu.VMEM((T, D), F32),
        pltpu.VMEM((T, D), F32),
        pltpu.VMEM((T, D), F32),
        pltpu.VMEM((T + 8, D), F32),
        pltpu.VMEM((3, 8, D), F32),
        pltpu.VMEM((T, HEAD_DIM), F32),
        pltpu.VMEM((T, HEAD_DIM), F32),
        pltpu.VMEM((T // DELTA_CHUNK, HEAD_DIM, DELTA_CHUNK), F32),
        pltpu.VMEM((T, D), F32),
        pltpu.VMEM((N_HEAD, HEAD_DIM, HEAD_DIM), F32),
        pltpu.VMEM((T // DELTA_CHUNK, N_HEAD, DELTA_CHUNK, DELTA_CHUNK), BF16),
        pltpu.VMEM((T // DELTA_CHUNK, N_HEAD, 2 * DELTA_CHUNK, HEAD_DIM), BF16),
        pltpu.VMEM((T, D), F32),
        pltpu.VMEM((T, D), BF16),
    ]
    return pl.pallas_call(
        functools.partial(_prompt_kernel, alpha),
        grid=(B, nt),
        in_specs=in_specs,
        out_specs=out_specs,
        out_shape=out_shape,
        scratch_shapes=scratch,
        compiler_params=pltpu.CompilerParams(
            dimension_semantics=("arbitrary", "arbitrary"),
            vmem_limit_bytes=VMEM_LIMIT_BYTES),
        name="prompt_layer",
    )(x, mod, *wts)


def _layer_weights(l, w_in, w_s, b_s, lnv_g, lnv_b, conv_w, a_log, dt_bias, onorm_g, w_pa, w_pb, w_o,
                   ln_g, ln_b):
    D = D_MODEL
    wi = w_in[l]
    ab0 = 7 * D
    wab = jnp.pad(wi[:, ab0:ab0 + 2 * N_HEAD], ((0, 0), (0, HEAD_DIM - 2 * N_HEAD)))
    lane_pad = ((0, 0), (N_HEAD, HEAD_DIM - 2 * N_HEAD))
    mats = dict(
        wuvz=wi[:, :3 * D].astype(BF16), wqkv=wi[:, 3 * D:6 * D].astype(BF16), wab=wab.astype(BF16),
        wzb=wi[:, 6 * D:7 * D].astype(BF16), wg=wi[:, ab0 + 2 * N_HEAD:].astype(BF16),
        wpa=w_pa[l].astype(BF16), wpb=w_pb[l].astype(BF16), wo=w_o[l].astype(BF16))
    small = dict(
        lnvg=lnv_g[l][None], lnvb=lnv_b[l][None], convw=conv_w[l],
        alog=jnp.pad(a_log[l][None], lane_pad), dtb=jnp.pad(dt_bias[l][None], lane_pad),
        onorm=jnp.tile(onorm_g[l], N_HEAD)[None], lng=ln_g[l][None], lnb=ln_b[l][None])
    return mats, small


def _prompt_operands(mats, small, w_s_l, b_s_l):
    bs_tile = jnp.repeat(b_s_l.T, HEAD_DIM, axis=1)
    return (mats["wuvz"], mats["wqkv"], mats["wab"], mats["wzb"], mats["wg"], mats["wpa"], mats["wpb"],
            mats["wo"], w_s_l, bs_tile, small["lnvg"], small["lnvb"], small["convw"], small["alog"],
            small["dtb"], small["onorm"], small["lng"], small["lnb"])


def _pairs(n, strict):
    return [(t, s) for t in range(n) for s in range(t if strict else t + 1)]


def _sample_front_kernel(n_step,
                         x_ref, mod_ref, cstate_ref, wuvz_ref, wqkv_ref, wab_ref, wsrow_ref, bs_ref,
                         lnvg_ref, lnvb_ref, convw_ref, alog_ref, dtb_ref,
                         h_ref, ya_ref, vrow_ref, conv_ref, u_ref, w_ref, qg_ref, kp_ref, qkd_ref, egl_ref,
                         q_s, k_s, v_s):
    D = D_MODEL
    NB = mod_ref.shape[0]
    L = n_step

    def slab(t):
        return slice(t * NB, (t + 1) * NB)

    mod = mod_ref[...]
    shift = jnp.concatenate([mod[:, :D]] * L, axis=0)
    scale = jnp.concatenate([mod[:, D:2 * D]] * L, axis=0)
    h_ref[...] = (_normalize(x_ref[...]) * (1.0 + scale) + shift).astype(BF16)
    hb = h_ref[...]

    vn = _normalize(_gelu_tanh(_dot(hb, wuvz_ref[:, D:2 * D]))) * lnvg_ref[...] + lnvb_ref[...]
    vrow_ref[...] = vn
    gate_rows = []
    for t in range(L):
        acc = wsrow_ref[t, 0:1, :] * vn[slab(0), :]
        for s in range(1, t + 1):
            acc = acc + wsrow_ref[t, s:s + 1, :] * vn[slab(s), :]
        gate_rows.append(acc + bs_ref[t:t + 1, :])
    sgate = jnp.concatenate(gate_rows, axis=0)
    gu = _gelu_tanh(_dot(hb, wuvz_ref[:, :D]))
    ya_ref[...] = ((gu * sgate) * _silu(_dot(hb, wuvz_ref[:, 2 * D:]))).astype(BF16)

    nc = CONV_W - 1
    for seg, dst in enumerate((q_s, k_s, v_s)):
        cols = slice(seg * D, (seg + 1) * D)
        p = _dot(hb, wqkv_ref[:, cols])
        window = [cstate_ref[j, :, cols] for j in range(nc)] + [p[slab(t), :] for t in range(L)]
        for j in range(nc):
            conv_ref[j, :, cols] = window[L + j]
        cw = convw_ref[:, cols]
        ys = []
        for t in range(L):
            y = cw[0:1, :] * window[t]
            for j in range(1, CONV_W):
                y = y + cw[j:j + 1, :] * window[t + j]
            ys.append(_silu(y))
        y = jnp.concatenate(ys, axis=0)
        if seg == 0:
            _l2norm_heads(y, dst, scale=HEAD_DIM ** -0.5)
        elif seg == 1:
            _l2norm_heads(y, dst)
        else:
            dst[...] = y

    pab = _dot(hb, wab_ref[...])
    beta = _sigmoid(pab)
    g = -jnp.exp(alog_ref[...]) * _softplus(pab + dtb_ref[...])
    gcs = [g[slab(0), :]]
    for t in range(1, L):
        gcs.append(gcs[-1] + g[slab(t), :])
    egl_ref[...] = jnp.exp(gcs[L - 1])

    lane = lax.broadcasted_iota(jnp.int32, (NB, HEAD_DIM), 1)
    qkd_acc = [jnp.zeros((NB, HEAD_DIM), F32) for _ in _pairs(L, False)]
    for h in range(N_HEAD):
        cols = _head_cols(h)
        q = [q_s[slab(t), cols] for t in range(L)]
        k = [k_s[slab(t), cols] for t in range(L)]
        v = [v_s[slab(t), cols] for t in range(L)]
        gcol = [jnp.broadcast_to(gcs[t][:, N_HEAD + h:N_HEAD + h + 1], (NB, HEAD_DIM)) for t in range(L)]
        bcol = [jnp.broadcast_to(beta[slab(t), h:h + 1], (NB, HEAD_DIM)) for t in range(L)]
        egc = [jnp.exp(gcol[t]) for t in range(L)]
        us, ws = [], []
        for t in range(L):
            u_t = v[t] * bcol[t]
            w_t = (k[t] * bcol[t]) * egc[t]
            for s in range(t):
                kk = jnp.sum(k[t] * k[s], axis=-1, keepdims=True)
                a_ts = (bcol[t] * kk) * jnp.exp(gcol[t] - gcol[s])
                u_t = u_t - a_ts * us[s]
                w_t = w_t - a_ts * ws[s]
            us.append(u_t)
            ws.append(w_t)
        for t in range(L):
            u_ref[slab(t), cols] = us[t]
            w_ref[slab(t), cols] = ws[t]
            qg_ref[slab(t), cols] = q[t] * egc[t]
            kp_ref[slab(t), cols] = k[t] * jnp.exp(gcol[L - 1] - gcol[t])
        for idx, (t, s) in enumerate(_pairs(L, False)):
            qk = jnp.sum(q[t] * k[s], axis=-1, keepdims=True)
            qkd = qk * jnp.exp(gcol[t] - gcol[s])
            qkd_acc[idx] = jnp.where(lane == h, qkd, qkd_acc[idx])
    for idx in range(len(qkd_acc)):
        qkd_ref[idx] = qkd_acc[idx]


def _sample_delta_kernel(n_step,
                         u_ref, w_ref, qg_ref, kp_ref, qkd_ref, egl_ref, state_ref,
                         o_ref, state_out_ref):
    L = n_step
    G = egl_ref.shape[0]
    row_lhs = lax.broadcasted_iota(jnp.int32, (2 * L * G, HEAD_DIM), 0) % G
    row_kp = lax.broadcasted_iota(jnp.int32, (L * G, HEAD_DIM), 0) % G
    pairs = _pairs(L, False)
    for h in range(N_HEAD):
        cols = _head_cols(h)
        lhs = jnp.concatenate([w_ref[t, :, cols] for t in range(L)]
                              + [qg_ref[t, :, cols] for t in range(L)], axis=0).astype(BF16)
        acc = jnp.zeros((2 * L * G, HEAD_DIM), F32)
        for b in range(G):
            res = _dot(lhs, state_ref[b, h].astype(BF16))
            acc = jnp.where(row_lhs == b, res, acc)
        v_new = [u_ref[t, :, cols] - acc[t * G:(t + 1) * G, :] for t in range(L)]
        for t in range(L):
            o_t = acc[(L + t) * G:(L + t + 1) * G, :]
            for s in range(t + 1):
                coef = jnp.broadcast_to(qkd_ref[pairs.index((t, s)), :, h:h + 1], (G, HEAD_DIM))
                o_t = o_t + coef * v_new[s]
            o_ref[t, :, cols] = o_t
        vn16 = jnp.concatenate(v_new, axis=0).astype(BF16)
        kpg = jnp.concatenate([kp_ref[t, :, cols] for t in range(L)], axis=0)
        for b in range(G):
            kpm = jnp.where(row_kp == b, kpg, 0.0).astype(BF16)
            eg = jnp.broadcast_to(egl_ref[b:b + 1, N_HEAD + h:N_HEAD + h + 1], (HEAD_DIM, HEAD_DIM))
            state_out_ref[b, h] = state_ref[b, h] * eg + _dot_tn(kpm, vn16)


def _sample_back_kernel(alpha, n_step,
                        x_ref, mod_ref, h_ref, ya_ref, o_ref, wzb_ref, wg_ref, wpa_ref, wpb_ref, wo_ref,
                        onorm_ref, lng_ref, lnb_ref, y_ref):
    D = D_MODEL
    gate = jnp.concatenate([mod_ref[:, 2 * D:]] * n_step, axis=0)
    hb = h_ref[...]
    yb = _rmsnorm_heads(o_ref[...], onorm_ref[...]) * _silu(_dot(hb, wzb_ref[...]))
    m = (_sigmoid(_dot(hb, wg_ref[:, :D])) * _dot(ya_ref[...], wpa_ref[...])
         + _sigmoid(_dot(hb, wg_ref[:, D:])) * _dot(yb.astype(BF16), wpb_ref[...]))
    out = _dot(m.astype(BF16), wo_ref[...])
    y_ref[...] = _normalize(alpha * x_ref[...] + gate * out) * lng_ref[...] + lnb_ref[...]


def _sample_layer(x, mod, cstate, state, mats, small, wsrow, bs_rows, alpha, n_step):
    R, D = x.shape
    L = n_step
    NB = R // L
    G = SAMPLE_GROUP
    assert NB % G == 0 and L >= CONV_W - 1
    npair = len(_pairs(L, False))
    params = pltpu.CompilerParams(vmem_limit_bytes=VMEM_LIMIT_BYTES)
    act = jax.ShapeDtypeStruct((R, D), F32)
    act16 = jax.ShapeDtypeStruct((R, D), BF16)
    h16, ya, vrows, conv_new, u, w, qg, kp, qkd, egl = pl.pallas_call(
        functools.partial(_sample_front_kernel, L),
        out_shape=[act16, act16, act, jax.ShapeDtypeStruct(cstate.shape, F32), act, act, act, act,
                   jax.ShapeDtypeStruct((npair, NB, HEAD_DIM), F32),
                   jax.ShapeDtypeStruct((NB, HEAD_DIM), F32)],
        scratch_shapes=[pltpu.VMEM((R, D), F32)] * 3,
        compiler_params=params,
        name="sample_front",
    )(x, mod, cstate, mats["wuvz"], mats["wqkv"], mats["wab"], wsrow, bs_rows,
      small["lnvg"], small["lnvb"], small["convw"], small["alog"], small["dtb"])

    def rows_spec():
        return pl.BlockSpec((L, G, D), lambda i: (0, i, 0))

    state_spec = pl.BlockSpec((G, N_HEAD, HEAD_DIM, HEAD_DIM), lambda i: (i, 0, 0, 0))
    as3 = lambda a: a.reshape(L, NB, D)
    o, state_new = pl.pallas_call(
        functools.partial(_sample_delta_kernel, L),
        grid=(NB // G,),
        in_specs=[rows_spec(), rows_spec(), rows_spec(), rows_spec(),
                  pl.BlockSpec((npair, G, HEAD_DIM), lambda i: (0, i, 0)),
                  pl.BlockSpec((G, HEAD_DIM), lambda i: (i, 0)),
                  state_spec],
        out_specs=[rows_spec(), state_spec],
        out_shape=[jax.ShapeDtypeStruct((L, NB, D), F32), jax.ShapeDtypeStruct(state.shape, F32)],
        compiler_params=pltpu.CompilerParams(
            dimension_semantics=("arbitrary",), vmem_limit_bytes=VMEM_LIMIT_BYTES),
        name="sample_delta",
    )(as3(u), as3(w), as3(qg), as3(kp), qkd, egl, state)
    y = pl.pallas_call(
        functools.partial(_sample_back_kernel, alpha, L),
        out_shape=act,
        compiler_params=params,
        name="sample_back",
    )(x, mod, h16, ya, o.reshape(R, D), mats["wzb"], mats["wg"], mats["wpa"], mats["wpb"], mats["wo"],
      small["onorm"], small["lng"], small["lnb"])
    return y, conv_new, state_new, vrows


def kernel(x_prompt, x_sample, state_conv, state_ssm, c_prompt, c_sample, w_ada, b_ada, w_in, w_s, b_s,
           lnv_g, lnv_b, conv_w, a_log, dt_bias, onorm_g, w_pa, w_pb, w_o, ln_g, ln_b):
    depth = w_in.shape[0]
    alpha = (2 * depth) ** 0.25
    bp = x_prompt.shape[0]
    nb, ls, d = x_sample.shape
    assert ls <= GMLP_CHUNK
    mod = _ada_mod(jnp.concatenate([c_prompt, c_sample], axis=0), w_ada, b_ada)
    xp = x_prompt
    xs = x_sample.transpose(1, 0, 2).reshape(ls * nb, d)
    outs = [[] for _ in range(6)]
    for l in range(depth):
        mats, small = _layer_weights(l, w_in, w_s, b_s, lnv_g, lnv_b, conv_w, a_log, dt_bias, onorm_g,
                                     w_pa, w_pb, w_o, ln_g, ln_b)
        xp, cp, sp, vp = _prompt_layer(xp, mod[l, :bp, None, :], _prompt_operands(mats, small, w_s[l], b_s[l]),
                                       alpha)
        wsrow = jnp.repeat(w_s[l][:, :ls, :ls].transpose(1, 2, 0), HEAD_DIM, axis=-1)
        bs_rows = jnp.repeat(b_s[l][:, :ls].T, HEAD_DIM, axis=-1)
        xs, cs, ss, vs = _sample_layer(xs, mod[l, bp:], state_conv[l].transpose(1, 0, 2), state_ssm[l],
                                       mats, small, wsrow, bs_rows, alpha, ls)
        for lst, val in zip(outs, (cp, sp, vp, cs.transpose(1, 0, 2), ss,
                                   vs.reshape(ls, nb, d).transpose(1, 0, 2))):
            lst.append(val)
    y_sample = xs.reshape(ls, nb, d).transpose(1, 0, 2)
    return (xp, y_sample) + tuple(jnp.stack(o) for o in outs)
```

```python
import functools

import jax
import jax.numpy as jnp
from jax import lax
from jax.experimental import pallas as pl
from jax.experimental.pallas import tpu as pltpu

D_MODEL = 1024
N_HEAD = 8
HEAD_DIM = 128
CONV_W = 4
GMLP_CHUNK = 128
DELTA_CHUNK = 128
NEUMANN_STEPS = DELTA_CHUNK.bit_length() - 2
LN_EPS = 1e-5
NORM_EPS = 1e-6
PROMPT_TILE = 256
SAMPLE_GROUP = 8
VMEM_LIMIT_BYTES = 56 * 1024 * 1024

F32 = jnp.float32
BF16 = jnp.bfloat16


def _dot(a, b):
    return jnp.dot(a, b, preferred_element_type=F32)


def _dot_nt(a, b):
    return lax.dot_general(a, b, (((1,), (1,)), ((), ())), preferred_element_type=F32)


def _dot_tn(a, b):
    return lax.dot_general(a, b, (((0,), (0,)), ((), ())), preferred_element_type=F32)


def _normalize(x):
    mu = jnp.mean(x, axis=-1, keepdims=True)
    xc = x - mu
    var = jnp.mean(xc * xc, axis=-1, keepdims=True)
    return xc * lax.rsqrt(var + LN_EPS)


def _gelu_tanh(x):
    return x * (0.5 * (1.0 + jnp.tanh(0.7978845608028654 * (x + 0.044715 * (x * x * x)))))


def _sigmoid(x):
    return 1.0 / (1.0 + jnp.exp(-x))


def _silu(x):
    return x * _sigmoid(x)


def _softplus(x):
    return jnp.maximum(x, 0.0) + jnp.log1p(jnp.exp(-jnp.abs(x)))


def _head_cols(h):
    return slice(h * HEAD_DIM, (h + 1) * HEAD_DIM)


def _l2norm_heads(y, dst_ref, scale=None):
    for h in range(N_HEAD):
        blk = y[:, _head_cols(h)]
        ss = jnp.sum(blk * blk, axis=-1, keepdims=True)
        out = blk * lax.rsqrt(ss + NORM_EPS)
        if scale is not None:
            out = out * scale
        dst_ref[:, _head_cols(h)] = out


def _rmsnorm_heads(o, gain_row):
    parts = []
    for h in range(N_HEAD):
        blk = o[:, _head_cols(h)]
        ms = jnp.mean(blk * blk, axis=-1, keepdims=True)
        parts.append(blk * lax.rsqrt(ms + NORM_EPS))
    return jnp.concatenate(parts, axis=1) * gain_row


def _split3_bf16(x):
    x1 = x.astype(BF16)
    r1 = x - x1.astype(F32)
    x2 = r1.astype(BF16)
    r2 = r1 - x2.astype(F32)
    return x1, x2, r2.astype(BF16)


def _ada_kernel(c_ref, w_ref, b_ref, o_ref):
    a = _silu(c_ref[...]).astype(BF16)
    o_ref[0] = _dot(a, w_ref[0].astype(BF16)) + b_ref[0]


def _ada_mod(c_all, w_ada, b_ada):
    depth, d, d3 = w_ada.shape
    n = c_all.shape[0]
    ncol = d3 // d
    return pl.pallas_call(
        _ada_kernel,
        grid=(depth, ncol),
        in_specs=[
            pl.BlockSpec((n, d), lambda l, j: (0, 0)),
            pl.BlockSpec((1, d, d), lambda l, j: (l, 0, j)),
            pl.BlockSpec((1, 1, d), lambda l, j: (l, 0, j)),
        ],
        out_specs=pl.BlockSpec((1, n, d), lambda l, j: (l, 0, j)),
        out_shape=jax.ShapeDtypeStruct((depth, n, d3), F32),
        compiler_params=pltpu.CompilerParams(
            dimension_semantics=("arbitrary", "arbitrary"),
            vmem_limit_bytes=VMEM_LIMIT_BYTES),
        name="ada_mod",
    )(c_all, w_ada, b_ada.reshape(depth, 1, d3))


def _prompt_kernel(alpha,
                   x_ref, mod_ref, wuvz_ref, wqkv_ref, wab_ref, wzb_ref, wg_ref, wpa_ref, wpb_ref,
                   wo_ref, ws_ref, bs_ref, lnvg_ref, lnvb_ref, convw_ref, alog_ref, dtb_ref,
                   onorm_ref, lng_ref, lnb_ref,
                   y_ref, conv_ref, ssm_ref, vrow_ref,
                   h_s, ya_s, q_s, k_s, v_s, cbuf_s, carry_s, gc_s, beta_s, gct_s, o_s, state_s,
                   qk_s, wq_s, u_s, kp_s, vn_s, sg_s, pa_s, pz_s, vnf_s, pe_s):
    D = D_MODEL
    T = x_ref.shape[1]
    i = pl.program_id(1)
    last = pl.num_programs(1) - 1
    n_gmlp = T // GMLP_CHUNK
    n_delta = T // DELTA_CHUNK
    C = DELTA_CHUNK

    @pl.when(i == 0)
    def _():
        state_s[...] = jnp.zeros_like(state_s)
        carry_s[...] = jnp.zeros_like(carry_s)

    x = x_ref[0]
    mod = mod_ref[0]
    shift, scale, gate = mod[:, :D], mod[:, D:2 * D], mod[:, 2 * D:]
    h_s[...] = (_normalize(x) * (1.0 + scale) + shift).astype(BF16)
    hb = h_s[...]

    row = lax.broadcasted_iota(jnp.int32, (C, C), 0)
    col = lax.broadcasted_iota(jnp.int32, (C, C), 1)
    tril_incl = row >= col
    tril_strict = row > col

    n_prob = n_delta * N_HEAD
    RB = T // n_prob
    assert RB % 16 == 0 and GMLP_CHUNK % RB == 0

    def branch_a_values(j):
        rb = slice(j * RB, (j + 1) * RB)
        vn = _normalize(_gelu_tanh(pa_s[rb, :])) * lnvg_ref[...] + lnvb_ref[...]
        if (j + 1) * RB > T - GMLP_CHUNK:
            vnf_s[j * RB - (T - GMLP_CHUNK):(j + 1) * RB - (T - GMLP_CHUNK), :] = vn
        vn_s[rb, :] = vn.astype(BF16)

    def branch_a_mix():
        for h in range(N_HEAD):
            wsh = jnp.where(tril_incl, ws_ref[h], 0.0).astype(BF16)
            for c in range(n_gmlp):
                rows = slice(c * GMLP_CHUNK, (c + 1) * GMLP_CHUNK)
                sg_s[rows, _head_cols(h)] = _dot(wsh, vn_s[rows, _head_cols(h)]) + bs_ref[:, _head_cols(h)]
        pa_s[...] = _dot(hb, wuvz_ref[:, :D])
        pz_s[...] = _dot(hb, wuvz_ref[:, 2 * D:])

    def branch_a_gate(j):
        rb = slice(j * RB, (j + 1) * RB)
        ya_s[rb, :] = ((_gelu_tanh(pa_s[rb, :]) * sg_s[rb, :]) * _silu(pz_s[rb, :])).astype(BF16)

    pa_s[...] = _dot(hb, wuvz_ref[:, D:2 * D])
    branch_a_with_step = {1: branch_a_values, 3: branch_a_gate}

    for seg, dst in enumerate((q_s, k_s, v_s)):
        cols = slice(seg * D, (seg + 1) * D)
        p = _dot(hb, wqkv_ref[:, cols])
        pe_s[seg] = _dot(hb, (wzb_ref[...], wg_ref[:, :D], wg_ref[:, D:])[seg])
        cbuf_s[0:8, :] = carry_s[seg]
        cbuf_s[8:8 + T, :] = p
        carry_s[seg] = cbuf_s[T:T + 8, :]
        cw = convw_ref[:, cols]
        y = cw[0:1, :] * cbuf_s[5:5 + T, :]
        y = y + cw[1:2, :] * cbuf_s[6:6 + T, :]
        y = y + cw[2:3, :] * cbuf_s[7:7 + T, :]
        y = y + cw[3:4, :] * p
        y = _silu(y)
        if seg == 0:
            _l2norm_heads(y, dst, scale=HEAD_DIM ** -0.5)
        elif seg == 1:
            _l2norm_heads(y, dst)
        else:
            dst[...] = y

    pab = _dot(hb, wab_ref[...])
    beta_s[...] = _sigmoid(pab)
    g = -jnp.exp(alog_ref[...]) * _softplus(pab + dtb_ref[...])
    rt = lax.broadcasted_iota(jnp.int32, (T, T), 0)
    ct = lax.broadcasted_iota(jnp.int32, (T, T), 1)
    ltri = jnp.where((rt // C == ct // C) & (ct <= rt), 1.0, 0.0).astype(BF16)
    g1, g2, g3 = _split3_bf16(g)
    gc = _dot(ltri, g1) + _dot(ltri, g2) + _dot(ltri, g3)
    gc_s[...] = gc
    for c in range(n_delta):
        gct_s[c] = gc[c * C:(c + 1) * C, :].T

    probs = [(c, h) for c in range(n_delta) for h in range(N_HEAD)]

    def rows_of(c):
        return slice(c * C, (c + 1) * C)

    def gcol_of(c, h):
        return jnp.broadcast_to(gc_s[rows_of(c), N_HEAD + h:N_HEAD + h + 1], (C, C))

    def bcol_of(c, h):
        return jnp.broadcast_to(beta_s[rows_of(c), h:h + 1], (C, C))

    grams = []
    for c, h in probs:
        k16 = k_s[rows_of(c), _head_cols(h)].astype(BF16)
        q16 = q_s[rows_of(c), _head_cols(h)].astype(BF16)
        grams.append(_dot_nt(jnp.concatenate([k16, q16], axis=0), k16))
    ys, pws = [], []
    for j, (c, h) in enumerate(probs):
        gcol = gcol_of(c, h)
        grow = gct_s[c, N_HEAD + h:N_HEAD + h + 1, :]
        decay = jnp.where(tril_incl, jnp.exp(gcol - grow), 0.0)
        xmat = jnp.where(tril_strict, -(grams[j][:C] * bcol_of(c, h) * decay), 0.0)
        qk_s[c, h] = jnp.where(tril_incl, grams[j][C:] * decay, 0.0).astype(BF16)
        ys.append(xmat)
        pws.append(xmat.astype(BF16))
    pw32s = [_dot(p, p) for p in pws]
    for step in range(1, NEUMANN_STEPS + 1):
        final = step == NEUMANN_STEPS
        for j in range(len(probs)):
            p16 = pw32s[j].astype(BF16)
            y16 = ys[j].astype(BF16)
            if final:
                ys[j] = ys[j] + pw32s[j] + _dot(p16, y16)
            else:
                z = _dot(p16, jnp.concatenate([p16, y16], axis=1))
                ys[j] = ys[j] + pw32s[j] + z[:, C:]
                pw32s[j] = z[:, :C]
            if step in branch_a_with_step:
                branch_a_with_step[step](j)
        if step == 2:
            branch_a_mix()
    for j, (c, h) in enumerate(probs):
        rows, cols = rows_of(c), _head_cols(h)
        gcol, bcol = gcol_of(c, h), bcol_of(c, h)
        egc = jnp.exp(gcol)
        k = k_s[rows, cols]
        rhs = jnp.concatenate([v_s[rows, cols] * bcol, (k * bcol) * egc], axis=1)
        uw = rhs + _dot(ys[j].astype(BF16), rhs.astype(BF16))
        u_s[rows, cols] = uw[:, :C]
        wq_s[c, h, 0:C, :] = uw[:, C:].astype(BF16)
        wq_s[c, h, C:2 * C, :] = (q_s[rows, cols] * egc).astype(BF16)
        kp_s[rows, cols] = (k * jnp.exp(gcol[C - 1:C, :] - gcol)).astype(BF16)

    for c in range(n_delta):
        rows = rows_of(c)
        prods = [_dot(wq_s[c, h], state_s[h].astype(BF16)) for h in range(N_HEAD)]
        v16s = []
        for h in range(N_HEAD):
            v16s.append((u_s[rows, _head_cols(h)] - prods[h][:C]).astype(BF16))
        for h in range(N_HEAD):
            o_s[rows, _head_cols(h)] = prods[h][C:] + _dot(qk_s[c, h], v16s[h])
        for h in range(N_HEAD):
            eg = jnp.exp(jnp.broadcast_to(gc_s[c * C + C - 1:c * C + C, N_HEAD + h:N_HEAD + h + 1], (C, C)))
            state_s[h] = state_s[h] * eg + _dot_tn(kp_s[rows, _head_cols(h)], v16s[h])

    yb = _rmsnorm_heads(o_s[...], onorm_ref[...]) * _silu(pe_s[0])
    m = (_sigmoid(pe_s[1]) * _dot(ya_s[...], wpa_ref[...])
         + _sigmoid(pe_s[2]) * _dot(yb.astype(BF16), wpb_ref[...]))
    out = _dot(m.astype(BF16), wo_ref[...])
    y_ref[0] = _normalize(alpha * x + gate * out) * lng_ref[...] + lnb_ref[...]

    @pl.when(i == last)
    def _():
        ssm_ref[0] = state_s[...]
        vrow_ref[0] = vnf_s[...]
        for seg in range(3):
            conv_ref[0, :, seg * D:(seg + 1) * D] = carry_s[seg, 8 - (CONV_W - 1):8, :]


def _resident_spec(shape):
    nd = len(shape)
    return pl.BlockSpec(shape, lambda *_: (0,) * nd, pipeline_mode=pl.Buffered(1))


def _prompt_layer(x, mod, wts, alpha):
    B, L, D = x.shape
    T = min(PROMPT_TILE, L)
    assert L % T == 0 and T % DELTA_CHUNK == 0 and T % GMLP_CHUNK == 0
    nt = L // T
    in_specs = [
        pl.BlockSpec((1, T, D), lambda b, i: (b, i, 0)),
        pl.BlockSpec((1, 1, 3 * D), lambda b, i: (b, 0, 0)),
    ] + [_resident_spec(w.shape) for w in wts]
    out_specs = [
        pl.BlockSpec((1, T, D), lambda b, i: (b, i, 0)),
        pl.BlockSpec((1, CONV_W - 1, 3 * D), lambda b, i: (b, 0, 0)),
        pl.BlockSpec((1, N_HEAD, HEAD_DIM, HEAD_DIM), lambda b, i: (b, 0, 0, 0)),
        pl.BlockSpec((1, GMLP_CHUNK, D), lambda b, i: (b, 0, 0)),
    ]
    out_shape = [
        jax.ShapeDtypeStruct((B, L, D), F32),
        jax.ShapeDtypeStruct((B, CONV_W - 1, 3 * D), F32),
        jax.ShapeDtypeStruct((B, N_HEAD, HEAD_DIM, HEAD_DIM), F32),
        jax.ShapeDtypeStruct((B, GMLP_CHUNK, D), F32),
    ]
    scratch = [
        pltpu.VMEM((T, D), BF16),
        pltpu.VMEM((T, D), BF16),
        pltpu.VMEM((T, D), F32),
        pltpu.VMEM((T, D), F32),
        pltpu.VMEM((T, D), F32),
        pltpu.VMEM((T + 8, D), F32),
        pltpu.VMEM((3, 8, D), F32),
        pltpu.VMEM((T, HEAD_DIM), F32),
        pltpu.VMEM((T, HEAD_DIM), F32),
        pltpu.VMEM((T // DELTA_CHUNK, HEAD_DIM, DELTA_CHUNK), F32),
        pltpu.VMEM((T, D), F32),
        pltpu.VMEM((N_HEAD, HEAD_DIM, HEAD_DIM), F32),
        pltpu.VMEM((T // DELTA_CHUNK, N_HEAD, DELTA_CHUNK, DELTA_CHUNK), BF16),
        pltpu.VMEM((T // DELTA_CHUNK, N_HEAD, 2 * DELTA_CHUNK, HEAD_DIM), BF16),
        pltpu.VMEM((T, D), F32),
        pltpu.VMEM((T, D), BF16),
        pltpu.VMEM((T, D), BF16),
        pltpu.VMEM((T, D), F32),
        pltpu.VMEM((T, D), F32),
        pltpu.VMEM((T, D), F32),
        pltpu.VMEM((GMLP_CHUNK, D), F32),
        pltpu.VMEM((3, T, D), F32),
    ]
    return pl.pallas_call(
        functools.partial(_prompt_kernel, alpha),
        grid=(B, nt),
        in_specs=in_specs,
        out_specs=out_specs,
        out_shape=out_shape,
        scratch_shapes=scratch,
        compiler_params=pltpu.CompilerParams(
            dimension_semantics=("arbitrary", "arbitrary"),
            vmem_limit_bytes=VMEM_LIMIT_BYTES),
        name="prompt_layer",
    )(x, mod, *wts)


def _layer_weights(l, w_in, w_s, b_s, lnv_g, lnv_b, conv_w, a_log, dt_bias, onorm_g, w_pa, w_pb, w_o,
                   ln_g, ln_b):
    D = D_MODEL
    wi = w_in[l]
    ab0 = 7 * D
    wab = jnp.pad(wi[:, ab0:ab0 + 2 * N_HEAD], ((0, 0), (0, HEAD_DIM - 2 * N_HEAD)))
    lane_pad = ((0, 0), (N_HEAD, HEAD_DIM - 2 * N_HEAD))
    mats = dict(
        wuvz=wi[:, :3 * D].astype(BF16), wqkv=wi[:, 3 * D:6 * D].astype(BF16), wab=wab.astype(BF16),
        wzb=wi[:, 6 * D:7 * D].astype(BF16), wg=wi[:, ab0 + 2 * N_HEAD:].astype(BF16),
        wpa=w_pa[l].astype(BF16), wpb=w_pb[l].astype(BF16), wo=w_o[l].astype(BF16))
    small = dict(
        lnvg=lnv_g[l][None], lnvb=lnv_b[l][None], convw=conv_w[l],
        alog=jnp.pad(a_log[l][None], lane_pad), dtb=jnp.pad(dt_bias[l][None], lane_pad),
        onorm=jnp.tile(onorm_g[l], N_HEAD)[None], lng=ln_g[l][None], lnb=ln_b[l][None])
    return mats, small


def _prompt_operands(mats, small, w_s_l, b_s_l):
    bs_tile = jnp.repeat(b_s_l.T, HEAD_DIM, axis=1)
    return (mats["wuvz"], mats["wqkv"], mats["wab"], mats["wzb"], mats["wg"], mats["wpa"], mats["wpb"],
            mats["wo"], w_s_l, bs_tile, small["lnvg"], small["lnvb"], small["convw"], small["alog"],
            small["dtb"], small["onorm"], small["lng"], small["lnb"])


def _pairs(n, strict):
    return [(t, s) for t in range(n) for s in range(t if strict else t + 1)]


def _sample_front_kernel(n_step,
                         x_ref, mod_ref, cstate_ref, wuvz_ref, wqkv_ref, wab_ref, wsrow_ref, bs_ref,
                         lnvg_ref, lnvb_ref, convw_ref, alog_ref, dtb_ref,
                         h_ref, ya_ref, vrow_ref, conv_ref, u_ref, w_ref, qg_ref, kp_ref, qkd_ref, egl_ref,
                         q_s, k_s, v_s):
    D = D_MODEL
    NB = mod_ref.shape[0]
    L = n_step

    def slab(t):
        return slice(t * NB, (t + 1) * NB)

    mod = mod_ref[...]
    shift = jnp.concatenate([mod[:, :D]] * L, axis=0)
    scale = jnp.concatenate([mod[:, D:2 * D]] * L, axis=0)
    h_ref[...] = (_normalize(x_ref[...]) * (1.0 + scale) + shift).astype(BF16)
    hb = h_ref[...]

    vn = _normalize(_gelu_tanh(_dot(hb, wuvz_ref[:, D:2 * D]))) * lnvg_ref[...] + lnvb_ref[...]
    vrow_ref[...] = vn
    gate_rows = []
    for t in range(L):
        acc = wsrow_ref[t, 0:1, :] * vn[slab(0), :]
        for s in range(1, t + 1):
            acc = acc + wsrow_ref[t, s:s + 1, :] * vn[slab(s), :]
        gate_rows.append(acc + bs_ref[t:t + 1, :])
    sgate = jnp.concatenate(gate_rows, axis=0)
    gu = _gelu_tanh(_dot(hb, wuvz_ref[:, :D]))
    ya_ref[...] = ((gu * sgate) * _silu(_dot(hb, wuvz_ref[:, 2 * D:]))).astype(BF16)

    nc = CONV_W - 1
    for seg, dst in enumerate((q_s, k_s, v_s)):
        cols = slice(seg * D, (seg + 1) * D)
        p = _dot(hb, wqkv_ref[:, cols])
        window = [cstate_ref[j, :, cols] for j in range(nc)] + [p[slab(t), :] for t in range(L)]
        for j in range(nc):
            conv_ref[j, :, cols] = window[L + j]
        cw = convw_ref[:, cols]
        ys = []
        for t in range(L):
            y = cw[0:1, :] * window[t]
            for j in range(1, CONV_W):
                y = y + cw[j:j + 1, :] * window[t + j]
            ys.append(_silu(y))
        y = jnp.concatenate(ys, axis=0)
        if seg == 0:
            _l2norm_heads(y, dst, scale=HEAD_DIM ** -0.5)
        elif seg == 1:
            _l2norm_heads(y, dst)
        else:
            dst[...] = y

    pab = _dot(hb, wab_ref[...])
    beta = _sigmoid(pab)
    g = -jnp.exp(alog_ref[...]) * _softplus(pab + dtb_ref[...])
    gcs = [g[slab(0), :]]
    for t in range(1, L):
        gcs.append(gcs[-1] + g[slab(t), :])
    egl_ref[...] = jnp.exp(gcs[L - 1])

    lane = lax.broadcasted_iota(jnp.int32, (NB, HEAD_DIM), 1)
    qkd_acc = [jnp.zeros((NB, HEAD_DIM), F32) for _ in _pairs(L, False)]
    for h in range(N_HEAD):
        cols = _head_cols(h)
        q = [q_s[slab(t), cols] for t in range(L)]
        k = [k_s[slab(t), cols] for t in range(L)]
        v = [v_s[slab(t), cols] for t in range(L)]
        gcol = [jnp.broadcast_to(gcs[t][:, N_HEAD + h:N_HEAD + h + 1], (NB, HEAD_DIM)) for t in range(L)]
        bcol = [jnp.broadcast_to(beta[slab(t), h:h + 1], (NB, HEAD_DIM)) for t in range(L)]
        egc = [jnp.exp(gcol[t]) for t in range(L)]
        us, ws = [], []
        for t in range(L):
            u_t = v[t] * bcol[t]
            w_t = (k[t] * bcol[t]) * egc[t]
            for s in range(t):
                kk = jnp.sum(k[t] * k[s], axis=-1, keepdims=True)
                a_ts = (bcol[t] * kk) * jnp.exp(gcol[t] - gcol[s])
                u_t = u_t - a_ts * us[s]
                w_t = w_t - a_ts * ws[s]
            us.append(u_t)
            ws.append(w_t)
        for t in range(L):
            u_ref[slab(t), cols] = us[t]
            w_ref[slab(t), cols] = ws[t]
            qg_ref[slab(t), cols] = q[t] * egc[t]
            kp_ref[slab(t), cols] = k[t] * jnp.exp(gcol[L - 1] - gcol[t])
        for idx, (t, s) in enumerate(_pairs(L, False)):
            qk = jnp.sum(q[t] * k[s], axis=-1, keepdims=True)
            qkd = qk * jnp.exp(gcol[t] - gcol[s])
            qkd_acc[idx] = jnp.where(lane == h, qkd, qkd_acc[idx])
    for idx in range(len(qkd_acc)):
        qkd_ref[idx] = qkd_acc[idx]


def _sample_delta_kernel(n_step,
                         u_ref, w_ref, qg_ref, kp_ref, qkd_ref, egl_ref, state_ref,
                         o_ref, state_out_ref):
    L = n_step
    G = egl_ref.shape[0]
    row_lhs = lax.broadcasted_iota(jnp.int32, (2 * L * G, HEAD_DIM), 0) % G
    row_kp = lax.broadcasted_iota(jnp.int32, (L * G, HEAD_DIM), 0) % G
    pairs = _pairs(L, False)
    for h in range(N_HEAD):
        cols = _head_cols(h)
        lhs = jnp.concatenate([w_ref[t, :, cols] for t in range(L)]
                              + [qg_ref[t, :, cols] for t in range(L)], axis=0).astype(BF16)
        acc = jnp.zeros((2 * L * G, HEAD_DIM), F32)
        for b in range(G):
            res = _dot(lhs, state_ref[b, h].astype(BF16))
            acc = jnp.where(row_lhs == b, res, acc)
        v_new = [u_ref[t, :, cols] - acc[t * G:(t + 1) * G, :] for t in range(L)]
        for t in range(L):
            o_t = acc[(L + t) * G:(L + t + 1) * G, :]
            for s in range(t + 1):
                coef = jnp.broadcast_to(qkd_ref[pairs.index((t, s)), :, h:h + 1], (G, HEAD_DIM))
                o_t = o_t + coef * v_new[s]
            o_ref[t, :, cols] = o_t
        vn16 = jnp.concatenate(v_new, axis=0).astype(BF16)
        kpg = jnp.concatenate([kp_ref[t, :, cols] for t in range(L)], axis=0)
        for b in range(G):
            kpm = jnp.where(row_kp == b, kpg, 0.0).astype(BF16)
            eg = jnp.broadcast_to(egl_ref[b:b + 1, N_HEAD + h:N_HEAD + h + 1], (HEAD_DIM, HEAD_DIM))
            state_out_ref[b, h] = state_ref[b, h] * eg + _dot_tn(kpm, vn16)


def _sample_back_kernel(alpha, n_step,
                        x_ref, mod_ref, h_ref, ya_ref, o_ref, wzb_ref, wg_ref, wpa_ref, wpb_ref, wo_ref,
                        onorm_ref, lng_ref, lnb_ref, y_ref):
    D = D_MODEL
    gate = jnp.concatenate([mod_ref[:, 2 * D:]] * n_step, axis=0)
    hb = h_ref[...]
    yb = _rmsnorm_heads(o_ref[...], onorm_ref[...]) * _silu(_dot(hb, wzb_ref[...]))
    m = (_sigmoid(_dot(hb, wg_ref[:, :D])) * _dot(ya_ref[...], wpa_ref[...])
         + _sigmoid(_dot(hb, wg_ref[:, D:])) * _dot(yb.astype(BF16), wpb_ref[...]))
    out = _dot(m.astype(BF16), wo_ref[...])
    y_ref[...] = _normalize(alpha * x_ref[...] + gate * out) * lng_ref[...] + lnb_ref[...]


def _sample_layer(x, mod, cstate, state, mats, small, wsrow, bs_rows, alpha, n_step):
    R, D = x.shape
    L = n_step
    NB = R // L
    G = SAMPLE_GROUP
    assert NB % G == 0 and L >= CONV_W - 1
    npair = len(_pairs(L, False))
    params = pltpu.CompilerParams(vmem_limit_bytes=VMEM_LIMIT_BYTES)
    act = jax.ShapeDtypeStruct((R, D), F32)
    act16 = jax.ShapeDtypeStruct((R, D), BF16)
    h16, ya, vrows, conv_new, u, w, qg, kp, qkd, egl = pl.pallas_call(
        functools.partial(_sample_front_kernel, L),
        out_shape=[act16, act16, act, jax.ShapeDtypeStruct(cstate.shape, F32), act, act, act, act,
                   jax.ShapeDtypeStruct((npair, NB, HEAD_DIM), F32),
                   jax.ShapeDtypeStruct((NB, HEAD_DIM), F32)],
        scratch_shapes=[pltpu.VMEM((R, D), F32)] * 3,
        compiler_params=params,
        name="sample_front",
    )(x, mod, cstate, mats["wuvz"], mats["wqkv"], mats["wab"], wsrow, bs_rows,
      small["lnvg"], small["lnvb"], small["convw"], small["alog"], small["dtb"])

    def rows_spec():
        return pl.BlockSpec((L, G, D), lambda i: (0, i, 0))

    state_spec = pl.BlockSpec((G, N_HEAD, HEAD_DIM, HEAD_DIM), lambda i: (i, 0, 0, 0))
    as3 = lambda a: a.reshape(L, NB, D)
    o, state_new = pl.pallas_call(
        functools.partial(_sample_delta_kernel, L),
        grid=(NB // G,),
        in_specs=[rows_spec(), rows_spec(), rows_spec(), rows_spec(),
                  pl.BlockSpec((npair, G, HEAD_DIM), lambda i: (0, i, 0)),
                  pl.BlockSpec((G, HEAD_DIM), lambda i: (i, 0)),
                  state_spec],
        out_specs=[rows_spec(), state_spec],
        out_shape=[jax.ShapeDtypeStruct((L, NB, D), F32), jax.ShapeDtypeStruct(state.shape, F32)],
        compiler_params=pltpu.CompilerParams(
            dimension_semantics=("arbitrary",), vmem_limit_bytes=VMEM_LIMIT_BYTES),
        name="sample_delta",
    )(as3(u), as3(w), as3(qg), as3(kp), qkd, egl, state)
    y = pl.pallas_call(
        functools.partial(_sample_back_kernel, alpha, L),
        out_shape=act,
        compiler_params=params,
        name="sample_back",
    )(x, mod, h16, ya, o.reshape(R, D), mats["wzb"], mats["wg"], mats["wpa"], mats["wpb"], mats["wo"],
      small["onorm"], small["lng"], small["lnb"])
    return y, conv_new, state_new, vrows


def kernel(x_prompt, x_sample, state_conv, state_ssm, c_prompt, c_sample, w_ada, b_ada, w_in, w_s, b_s,
           lnv_g, lnv_b, conv_w, a_log, dt_bias, onorm_g, w_pa, w_pb, w_o, ln_g, ln_b):
    depth = w_in.shape[0]
    alpha = (2 * depth) ** 0.25
    bp = x_prompt.shape[0]
    nb, ls, d = x_sample.shape
    assert ls <= GMLP_CHUNK
    mod = _ada_mod(jnp.concatenate([c_prompt, c_sample], axis=0), w_ada, b_ada)
    xp = x_prompt
    xs = x_sample.transpose(1, 0, 2).reshape(ls * nb, d)
    outs = [[] for _ in range(6)]
    for l in range(depth):
        mats, small = _layer_weights(l, w_in, w_s, b_s, lnv_g, lnv_b, conv_w, a_log, dt_bias, onorm_g,
                                     w_pa, w_pb, w_o, ln_g, ln_b)
        xp, cp, sp, vp = _prompt_layer(xp, mod[l, :bp, None, :], _prompt_operands(mats, small, w_s[l], b_s[l]),
                                       alpha)
        wsrow = jnp.repeat(w_s[l][:, :ls, :ls].transpose(1, 2, 0), HEAD_DIM, axis=-1)
        bs_rows = jnp.repeat(b_s[l][:, :ls].T, HEAD_DIM, axis=-1)
        xs, cs, ss, vs = _sample_layer(xs, mod[l, bp:], state_conv[l].transpose(1, 0, 2), state_ssm[l],
                                       mats, small, wsrow, bs_rows, alpha, ls)
        for lst, val in zip(outs, (cp, sp, vp, cs.transpose(1, 0, 2), ss,
                                   vs.reshape(ls, nb, d).transpose(1, 0, 2))):
            lst.append(val)
    y_sample = xs.reshape(ls, nb, d).transpose(1, 0, 2)
    return (xp, y_sample) + tuple(jnp.stack(o) for o in outs)
```

```python
import functools

import jax
import jax.numpy as jnp
from jax import lax
from jax.experimental import pallas as pl
from jax.experimental.pallas import tpu as pltpu

D_MODEL = 1024
N_HEAD = 8
HEAD_DIM = 128
CONV_W = 4
GMLP_CHUNK = 128
DELTA_CHUNK = 128
NEUMANN_STEPS = DELTA_CHUNK.bit_length() - 2
LN_EPS = 1e-5
NORM_EPS = 1e-6
PROMPT_TILE = 256
SAMPLE_GROUP = 8
VMEM_LIMIT_BYTES = 56 * 1024 * 1024

F32 = jnp.float32
BF16 = jnp.bfloat16


def _dot(a, b):
    return jnp.dot(a, b, preferred_element_type=F32)


def _dot_nt(a, b):
    return lax.dot_general(a, b, (((1,), (1,)), ((), ())), preferred_element_type=F32)


def _dot_tn(a, b):
    return lax.dot_general(a, b, (((0,), (0,)), ((), ())), preferred_element_type=F32)


def _normalize(x):
    mu = jnp.mean(x, axis=-1, keepdims=True)
    xc = x - mu
    var = jnp.mean(xc * xc, axis=-1, keepdims=True)
    return xc * lax.rsqrt(var + LN_EPS)


def _gelu_tanh(x):
    return x * (0.5 * (1.0 + jnp.tanh(0.7978845608028654 * (x + 0.044715 * (x * x * x)))))


def _sigmoid(x):
    return 1.0 / (1.0 + jnp.exp(-x))


def _silu(x):
    return x * _sigmoid(x)


def _softplus(x):
    return jnp.maximum(x, 0.0) + jnp.log1p(jnp.exp(-jnp.abs(x)))


def _head_cols(h):
    return slice(h * HEAD_DIM, (h + 1) * HEAD_DIM)


def _l2norm_heads(y, dst_ref, scale=None):
    for h in range(N_HEAD):
        blk = y[:, _head_cols(h)]
        ss = jnp.sum(blk * blk, axis=-1, keepdims=True)
        out = blk * lax.rsqrt(ss + NORM_EPS)
        if scale is not None:
            out = out * scale
        dst_ref[:, _head_cols(h)] = out


def _rmsnorm_heads(o, gain_row):
    parts = []
    for h in range(N_HEAD):
        blk = o[:, _head_cols(h)]
        ms = jnp.mean(blk * blk, axis=-1, keepdims=True)
        parts.append(blk * lax.rsqrt(ms + NORM_EPS))
    return jnp.concatenate(parts, axis=1) * gain_row


def _split3_bf16(x):
    x1 = x.astype(BF16)
    r1 = x - x1.astype(F32)
    x2 = r1.astype(BF16)
    r2 = r1 - x2.astype(F32)
    return x1, x2, r2.astype(BF16)


def _ada_kernel(c_ref, w_ref, b_ref, o_ref):
    a = _silu(c_ref[...]).astype(BF16)
    o_ref[0] = _dot(a, w_ref[0].astype(BF16)) + b_ref[0]


def _ada_mod(c_all, w_ada, b_ada):
    depth, d, d3 = w_ada.shape
    n = c_all.shape[0]
    ncol = d3 // d
    return pl.pallas_call(
        _ada_kernel,
        grid=(depth, ncol),
        in_specs=[
            pl.BlockSpec((n, d), lambda l, j: (0, 0)),
            pl.BlockSpec((1, d, d), lambda l, j: (l, 0, j)),
            pl.BlockSpec((1, 1, d), lambda l, j: (l, 0, j)),
        ],
        out_specs=pl.BlockSpec((1, n, d), lambda l, j: (l, 0, j)),
        out_shape=jax.ShapeDtypeStruct((depth, n, d3), F32),
        compiler_params=pltpu.CompilerParams(
            dimension_semantics=("arbitrary", "arbitrary"),
            vmem_limit_bytes=VMEM_LIMIT_BYTES),
        name="ada_mod",
    )(c_all, w_ada, b_ada.reshape(depth, 1, d3))


def _prompt_kernel(alpha,
                   x_ref, mod_ref, wuvz_ref, wqkv_ref, wab_ref, wzb_ref, wg_ref, wpa_ref, wpb_ref,
                   wo_ref, ws_ref, bs_ref, lnvg_ref, lnvb_ref, convw_ref, alog_ref, dtb_ref,
                   onorm_ref, lng_ref, lnb_ref,
                   y_ref, conv_ref, ssm_ref, vrow_ref,
                   h_s, ya_s, q_s, k_s, v_s, cbuf_s, carry_s, gc_s, beta_s, gct_s, o_s, state_s,
                   qk_s, wq_s, u_s, kp_s, vn_s, sg_s, pa_s, pz_s, vnf_s, pe_s):
    D = D_MODEL
    T = x_ref.shape[1]
    i = pl.program_id(1)
    last = pl.num_programs(1) - 1
    n_gmlp = T // GMLP_CHUNK
    n_delta = T // DELTA_CHUNK
    C = DELTA_CHUNK

    @pl.when(i == 0)
    def _():
        state_s[...] = jnp.zeros_like(state_s)
        carry_s[...] = jnp.zeros_like(carry_s)

    x = x_ref[0]
    mod = mod_ref[0]
    shift, scale, gate = mod[:, :D], mod[:, D:2 * D], mod[:, 2 * D:]
    h_s[...] = (_normalize(x) * (1.0 + scale) + shift).astype(BF16)
    hb = h_s[...]

    row = lax.broadcasted_iota(jnp.int32, (C, C), 0)
    col = lax.broadcasted_iota(jnp.int32, (C, C), 1)
    tril_incl = row >= col
    tril_strict = row > col

    n_prob = n_delta * N_HEAD
    RB = T // n_prob
    assert RB % 16 == 0 and GMLP_CHUNK % RB == 0

    def branch_a_values(j):
        rb = slice(j * RB, (j + 1) * RB)
        vn = _normalize(_gelu_tanh(pa_s[rb, :])) * lnvg_ref[...] + lnvb_ref[...]
        if (j + 1) * RB > T - GMLP_CHUNK:
            vnf_s[j * RB - (T - GMLP_CHUNK):(j + 1) * RB - (T - GMLP_CHUNK), :] = vn
        vn_s[rb, :] = vn.astype(BF16)

    def branch_a_mix():
        for h in range(N_HEAD):
            wsh = jnp.where(tril_incl, ws_ref[h], 0.0).astype(BF16)
            for c in range(n_gmlp):
                rows = slice(c * GMLP_CHUNK, (c + 1) * GMLP_CHUNK)
                sg_s[rows, _head_cols(h)] = _dot(wsh, vn_s[rows, _head_cols(h)]) + bs_ref[:, _head_cols(h)]

    def branch_a_gate(j):
        rb = slice(j * RB, (j + 1) * RB)
        ya_s[rb, :] = ((_gelu_tanh(pe_s[3, rb, :]) * sg_s[rb, :]) * _silu(pe_s[4, rb, :])).astype(BF16)

    pa_s[...] = _dot(hb, wuvz_ref[:, D:2 * D])
    branch_a_with_step = {1: branch_a_values, 3: branch_a_gate}

    for seg, dst in enumerate((q_s, k_s, v_s)):
        cols = slice(seg * D, (seg + 1) * D)
        p = _dot(hb, wqkv_ref[:, cols])
        pe_s[seg] = _dot(hb, (wzb_ref[...], wg_ref[:, :D], wg_ref[:, D:])[seg])
        if seg < 2:
            pe_s[3 + seg] = _dot(hb, (wuvz_ref[:, :D], wuvz_ref[:, 2 * D:])[seg])
        cbuf_s[0:8, :] = carry_s[seg]
        cbuf_s[8:8 + T, :] = p
        carry_s[seg] = cbuf_s[T:T + 8, :]
        cw = convw_ref[:, cols]
        y = cw[0:1, :] * cbuf_s[5:5 + T, :]
        y = y + cw[1:2, :] * cbuf_s[6:6 + T, :]
        y = y + cw[2:3, :] * cbuf_s[7:7 + T, :]
        y = y + cw[3:4, :] * p
        y = _silu(y)
        if seg == 0:
            _l2norm_heads(y, dst, scale=HEAD_DIM ** -0.5)
        elif seg == 1:
            _l2norm_heads(y, dst)
        else:
            dst[...] = y

    pab = _dot(hb, wab_ref[...])
    beta_s[...] = _sigmoid(pab)
    g = -jnp.exp(alog_ref[...]) * _softplus(pab + dtb_ref[...])
    rt = lax.broadcasted_iota(jnp.int32, (T, T), 0)
    ct = lax.broadcasted_iota(jnp.int32, (T, T), 1)
    ltri = jnp.where((rt // C == ct // C) & (ct <= rt), 1.0, 0.0).astype(BF16)
    g1, g2, g3 = _split3_bf16(g)
    gc = _dot(ltri, g1) + _dot(ltri, g2) + _dot(ltri, g3)
    gc_s[...] = gc
    for c in range(n_delta):
        gct_s[c] = gc[c * C:(c + 1) * C, :].T

    probs = [(c, h) for c in range(n_delta) for h in range(N_HEAD)]

    def rows_of(c):
        return slice(c * C, (c + 1) * C)

    def gcol_of(c, h):
        return jnp.broadcast_to(gc_s[rows_of(c), N_HEAD + h:N_HEAD + h + 1], (C, C))

    def bcol_of(c, h):
        return jnp.broadcast_to(beta_s[rows_of(c), h:h + 1], (C, C))

    grams = []
    for c, h in probs:
        k16 = k_s[rows_of(c), _head_cols(h)].astype(BF16)
        q16 = q_s[rows_of(c), _head_cols(h)].astype(BF16)
        grams.append(_dot_nt(jnp.concatenate([k16, q16], axis=0), k16))
    ys, pws = [], []
    for j, (c, h) in enumerate(probs):
        gcol = gcol_of(c, h)
        grow = gct_s[c, N_HEAD + h:N_HEAD + h + 1, :]
        decay = jnp.where(tril_incl, jnp.exp(gcol - grow), 0.0)
        xmat = jnp.where(tril_strict, -(grams[j][:C] * bcol_of(c, h) * decay), 0.0)
        qk_s[c, h] = jnp.where(tril_incl, grams[j][C:] * decay, 0.0).astype(BF16)
        ys.append(xmat)
        pws.append(xmat.astype(BF16))
    pw32s = [_dot(p, p) for p in pws]
    for step in range(1, NEUMANN_STEPS + 1):
        final = step == NEUMANN_STEPS
        for j in range(len(probs)):
            p16 = pw32s[j].astype(BF16)
            y16 = ys[j].astype(BF16)
            if final:
                ys[j] = ys[j] + pw32s[j] + _dot(p16, y16)
            else:
                z = _dot(p16, jnp.concatenate([p16, y16], axis=1))
                ys[j] = ys[j] + pw32s[j] + z[:, C:]
                pw32s[j] = z[:, :C]
            if step in branch_a_with_step:
                branch_a_with_step[step](j)
        if step == 2:
            branch_a_mix()
    for j, (c, h) in enumerate(probs):
        rows, cols = rows_of(c), _head_cols(h)
        gcol, bcol = gcol_of(c, h), bcol_of(c, h)
        egc = jnp.exp(gcol)
        k = k_s[rows, cols]
        rhs = jnp.concatenate([v_s[rows, cols] * bcol, (k * bcol) * egc], axis=1)
        uw = rhs + _dot(ys[j].astype(BF16), rhs.astype(BF16))
        u_s[rows, cols] = uw[:, :C]
        wq_s[c, h, 0:C, :] = uw[:, C:].astype(BF16)
        wq_s[c, h, C:2 * C, :] = (q_s[rows, cols] * egc).astype(BF16)
        kp_s[rows, cols] = (k * jnp.exp(gcol[C - 1:C, :] - gcol)).astype(BF16)

    for c in range(n_delta):
        rows = rows_of(c)
        prods = [_dot(wq_s[c, h], state_s[h].astype(BF16)) for h in range(N_HEAD)]
        v16s = []
        for h in range(N_HEAD):
            v16s.append((u_s[rows, _head_cols(h)] - prods[h][:C]).astype(BF16))
        for h in range(N_HEAD):
            o_s[rows, _head_cols(h)] = prods[h][C:] + _dot(qk_s[c, h], v16s[h])
        for h in range(N_HEAD):
            eg = jnp.exp(jnp.broadcast_to(gc_s[c * C + C - 1:c * C + C, N_HEAD + h:N_HEAD + h + 1], (C, C)))
            state_s[h] = state_s[h] * eg + _dot_tn(kp_s[rows, _head_cols(h)], v16s[h])

    yb = _rmsnorm_heads(o_s[...], onorm_ref[...]) * _silu(pe_s[0])
    m = (_sigmoid(pe_s[1]) * _dot(ya_s[...], wpa_ref[...])
         + _sigmoid(pe_s[2]) * _dot(yb.astype(BF16), wpb_ref[...]))
    out = _dot(m.astype(BF16), wo_ref[...])
    y_ref[0] = _normalize(alpha * x + gate * out) * lng_ref[...] + lnb_ref[...]

    @pl.when(i == last)
    def _():
        ssm_ref[0] = state_s[...]
        vrow_ref[0] = vnf_s[...]
        for seg in range(3):
            conv_ref[0, :, seg * D:(seg + 1) * D] = carry_s[seg, 8 - (CONV_W - 1):8, :]


def _resident_spec(shape):
    nd = len(shape)
    return pl.BlockSpec(shape, lambda *_: (0,) * nd, pipeline_mode=pl.Buffered(1))


def _prompt_layer(x, mod, wts, alpha):
    B, L, D = x.shape
    T = min(PROMPT_TILE, L)
    assert L % T == 0 and T % DELTA_CHUNK == 0 and T % GMLP_CHUNK == 0
    nt = L // T
    in_specs = [
        pl.BlockSpec((1, T, D), lambda b, i: (b, i, 0)),
        pl.BlockSpec((1, 1, 3 * D), lambda b, i: (b, 0, 0)),
    ] + [_resident_spec(w.shape) for w in wts]
    out_specs = [
        pl.BlockSpec((1, T, D), lambda b, i: (b, i, 0)),
        pl.BlockSpec((1, CONV_W - 1, 3 * D), lambda b, i: (b, 0, 0)),
        pl.BlockSpec((1, N_HEAD, HEAD_DIM, HEAD_DIM), lambda b, i: (b, 0, 0, 0)),
        pl.BlockSpec((1, GMLP_CHUNK, D), lambda b, i: (b, 0, 0)),
    ]
    out_shape = [
        jax.ShapeDtypeStruct((B, L, D), F32),
        jax.ShapeDtypeStruct((B, CONV_W - 1, 3 * D), F32),
        jax.ShapeDtypeStruct((B, N_HEAD, HEAD_DIM, HEAD_DIM), F32),
        jax.ShapeDtypeStruct((B, GMLP_CHUNK, D), F32),
    ]
    scratch = [
        pltpu.VMEM((T, D), BF16),
        pltpu.VMEM((T, D), BF16),
        pltpu.VMEM((T, D), F32),
        pltpu.VMEM((T, D), F32),
        pltpu.VMEM((T, D), F32),
        pltpu.VMEM((T + 8, D), F32),
        pltpu.VMEM((3, 8, D), F32),
        pltpu.VMEM((T, HEAD_DIM), F32),
        pltpu.VMEM((T, HEAD_DIM), F32),
        pltpu.VMEM((T // DELTA_CHUNK, HEAD_DIM, DELTA_CHUNK), F32),
        pltpu.VMEM((T, D), F32),
        pltpu.VMEM((N_HEAD, HEAD_DIM, HEAD_DIM), F32),
        pltpu.VMEM((T // DELTA_CHUNK, N_HEAD, DELTA_CHUNK, DELTA_CHUNK), BF16),
        pltpu.VMEM((T // DELTA_CHUNK, N_HEAD, 2 * DELTA_CHUNK, HEAD_DIM), BF16),
        pltpu.VMEM((T, D), F32),
        pltpu.VMEM((T, D), BF16),
        pltpu.VMEM((T, D), BF16),
        pltpu.VMEM((T, D), F32),
        pltpu.VMEM((T, D), F32),
        pltpu.VMEM((T, D), F32),
        pltpu.VMEM((GMLP_CHUNK, D), F32),
        pltpu.VMEM((5, T, D), F32),
    ]
    return pl.pallas_call(
        functools.partial(_prompt_kernel, alpha),
        grid=(B, nt),
        in_specs=in_specs,
        out_specs=out_specs,
        out_shape=out_shape,
        scratch_shapes=scratch,
        compiler_params=pltpu.CompilerParams(
            dimension_semantics=("arbitrary", "arbitrary"),
            vmem_limit_bytes=VMEM_LIMIT_BYTES),
        name="prompt_layer",
    )(x, mod, *wts)


def _layer_weights(l, w_in, w_s, b_s, lnv_g, lnv_b, conv_w, a_log, dt_bias, onorm_g, w_pa, w_pb, w_o,
                   ln_g, ln_b):
    D = D_MODEL
    wi = w_in[l]
    ab0 = 7 * D
    wab = jnp.pad(wi[:, ab0:ab0 + 2 * N_HEAD], ((0, 0), (0, HEAD_DIM - 2 * N_HEAD)))
    lane_pad = ((0, 0), (N_HEAD, HEAD_DIM - 2 * N_HEAD))
    mats = dict(
        wuvz=wi[:, :3 * D].astype(BF16), wqkv=wi[:, 3 * D:6 * D].astype(BF16), wab=wab.astype(BF16),
        wzb=wi[:, 6 * D:7 * D].astype(BF16), wg=wi[:, ab0 + 2 * N_HEAD:].astype(BF16),
        wpa=w_pa[l].astype(BF16), wpb=w_pb[l].astype(BF16), wo=w_o[l].astype(BF16))
    small = dict(
        lnvg=lnv_g[l][None], lnvb=lnv_b[l][None], convw=conv_w[l],
        alog=jnp.pad(a_log[l][None], lane_pad), dtb=jnp.pad(dt_bias[l][None], lane_pad),
        onorm=jnp.tile(onorm_g[l], N_HEAD)[None], lng=ln_g[l][None], lnb=ln_b[l][None])
    return mats, small


def _prompt_operands(mats, small, w_s_l, b_s_l):
    bs_tile = jnp.repeat(b_s_l.T, HEAD_DIM, axis=1)
    return (mats["wuvz"], mats["wqkv"], mats["wab"], mats["wzb"], mats["wg"], mats["wpa"], mats["wpb"],
            mats["wo"], w_s_l, bs_tile, small["lnvg"], small["lnvb"], small["convw"], small["alog"],
            small["dtb"], small["onorm"], small["lng"], small["lnb"])


def _pairs(n, strict):
    return [(t, s) for t in range(n) for s in range(t if strict else t + 1)]


def _sample_front_kernel(n_step,
                         x_ref, mod_ref, cstate_ref, wuvz_ref, wqkv_ref, wab_ref, wsrow_ref, bs_ref,
                         lnvg_ref, lnvb_ref, convw_ref, alog_ref, dtb_ref,
                         h_ref, ya_ref, vrow_ref, conv_ref, u_ref, w_ref, qg_ref, kp_ref, qkd_ref, egl_ref,
                         q_s, k_s, v_s):
    D = D_MODEL
    NB = mod_ref.shape[0]
    L = n_step

    def slab(t):
        return slice(t * NB, (t + 1) * NB)

    mod = mod_ref[...]
    shift = jnp.concatenate([mod[:, :D]] * L, axis=0)
    scale = jnp.concatenate([mod[:, D:2 * D]] * L, axis=0)
    h_ref[...] = (_normalize(x_ref[...]) * (1.0 + scale) + shift).astype(BF16)
    hb = h_ref[...]

    vn = _normalize(_gelu_tanh(_dot(hb, wuvz_ref[:, D:2 * D]))) * lnvg_ref[...] + lnvb_ref[...]
    vrow_ref[...] = vn
    gate_rows = []
    for t in range(L):
        acc = wsrow_ref[t, 0:1, :] * vn[slab(0), :]
        for s in range(1, t + 1):
            acc = acc + wsrow_ref[t, s:s + 1, :] * vn[slab(s), :]
        gate_rows.append(acc + bs_ref[t:t + 1, :])
    sgate = jnp.concatenate(gate_rows, axis=0)
    gu = _gelu_tanh(_dot(hb, wuvz_ref[:, :D]))
    ya_ref[...] = ((gu * sgate) * _silu(_dot(hb, wuvz_ref[:, 2 * D:]))).astype(BF16)

    nc = CONV_W - 1
    for seg, dst in enumerate((q_s, k_s, v_s)):
        cols = slice(seg * D, (seg + 1) * D)
        p = _dot(hb, wqkv_ref[:, cols])
        window = [cstate_ref[j, :, cols] for j in range(nc)] + [p[slab(t), :] for t in range(L)]
        for j in range(nc):
            conv_ref[j, :, cols] = window[L + j]
        cw = convw_ref[:, cols]
        ys = []
        for t in range(L):
            y = cw[0:1, :] * window[t]
            for j in range(1, CONV_W):
                y = y + cw[j:j + 1, :] * window[t + j]
            ys.append(_silu(y))
        y = jnp.concatenate(ys, axis=0)
        if seg == 0:
            _l2norm_heads(y, dst, scale=HEAD_DIM ** -0.5)
        elif seg == 1:
            _l2norm_heads(y, dst)
        else:
            dst[...] = y

    pab = _dot(hb, wab_ref[...])
    beta = _sigmoid(pab)
    g = -jnp.exp(alog_ref[...]) * _softplus(pab + dtb_ref[...])
    gcs = [g[slab(0), :]]
    for t in range(1, L):
        gcs.append(gcs[-1] + g[slab(t), :])
    egl_ref[...] = jnp.exp(gcs[L - 1])

    lane = lax.broadcasted_iota(jnp.int32, (NB, HEAD_DIM), 1)
    qkd_acc = [jnp.zeros((NB, HEAD_DIM), F32) for _ in _pairs(L, False)]
    for h in range(N_HEAD):
        cols = _head_cols(h)
        q = [q_s[slab(t), cols] for t in range(L)]
        k = [k_s[slab(t), cols] for t in range(L)]
        v = [v_s[slab(t), cols] for t in range(L)]
        gcol = [jnp.broadcast_to(gcs[t][:, N_HEAD + h:N_HEAD + h + 1], (NB, HEAD_DIM)) for t in range(L)]
        bcol = [jnp.broadcast_to(beta[slab(t), h:h + 1], (NB, HEAD_DIM)) for t in range(L)]
        egc = [jnp.exp(gcol[t]) for t in range(L)]
        us, ws = [], []
        for t in range(L):
            u_t = v[t] * bcol[t]
            w_t = (k[t] * bcol[t]) * egc[t]
            for s in range(t):
                kk = jnp.sum(k[t] * k[s], axis=-1, keepdims=True)
                a_ts = (bcol[t] * kk) * jnp.exp(gcol[t] - gcol[s])
                u_t = u_t - a_ts * us[s]
                w_t = w_t - a_ts * ws[s]
            us.append(u_t)
            ws.append(w_t)
        for t in range(L):
            u_ref[slab(t), cols] = us[t]
            w_ref[slab(t), cols] = ws[t]
            qg_ref[slab(t), cols] = q[t] * egc[t]
            kp_ref[slab(t), cols] = k[t] * jnp.exp(gcol[L - 1] - gcol[t])
        for idx, (t, s) in enumerate(_pairs(L, False)):
            qk = jnp.sum(q[t] * k[s], axis=-1, keepdims=True)
            qkd = qk * jnp.exp(gcol[t] - gcol[s])
            qkd_acc[idx] = jnp.where(lane == h, qkd, qkd_acc[idx])
    for idx in range(len(qkd_acc)):
        qkd_ref[idx] = qkd_acc[idx]


def _sample_delta_kernel(n_step,
                         u_ref, w_ref, qg_ref, kp_ref, qkd_ref, egl_ref, state_ref,
                         o_ref, state_out_ref):
    L = n_step
    G = egl_ref.shape[0]
    row_lhs = lax.broadcasted_iota(jnp.int32, (2 * L * G, HEAD_DIM), 0) % G
    row_kp = lax.broadcasted_iota(jnp.int32, (L * G, HEAD_DIM), 0) % G
    pairs = _pairs(L, False)
    for h in range(N_HEAD):
        cols = _head_cols(h)
        lhs = jnp.concatenate([w_ref[t, :, cols] for t in range(L)]
                              + [qg_ref[t, :, cols] for t in range(L)], axis=0).astype(BF16)
        acc = jnp.zeros((2 * L * G, HEAD_DIM), F32)
        for b in range(G):
            res = _dot(lhs, state_ref[b, h].astype(BF16))
            acc = jnp.where(row_lhs == b, res, acc)
        v_new = [u_ref[t, :, cols] - acc[t * G:(t + 1) * G, :] for t in range(L)]
        for t in range(L):
            o_t = acc[(L + t) * G:(L + t + 1) * G, :]
            for s in range(t + 1):
                coef = jnp.broadcast_to(qkd_ref[pairs.index((t, s)), :, h:h + 1], (G, HEAD_DIM))
                o_t = o_t + coef * v_new[s]
            o_ref[t, :, cols] = o_t
        vn16 = jnp.concatenate(v_new, axis=0).astype(BF16)
        kpg = jnp.concatenate([kp_ref[t, :, cols] for t in range(L)], axis=0)
        for b in range(G):
            kpm = jnp.where(row_kp == b, kpg, 0.0).astype(BF16)
            eg = jnp.broadcast_to(egl_ref[b:b + 1, N_HEAD + h:N_HEAD + h + 1], (HEAD_DIM, HEAD_DIM))
            state_out_ref[b, h] = state_ref[b, h] * eg + _dot_tn(kpm, vn16)


def _sample_back_kernel(alpha, n_step,
                        x_ref, mod_ref, h_ref, ya_ref, o_ref, wzb_ref, wg_ref, wpa_ref, wpb_ref, wo_ref,
                        onorm_ref, lng_ref, lnb_ref, y_ref):
    D = D_MODEL
    gate = jnp.concatenate([mod_ref[:, 2 * D:]] * n_step, axis=0)
    hb = h_ref[...]
    yb = _rmsnorm_heads(o_ref[...], onorm_ref[...]) * _silu(_dot(hb, wzb_ref[...]))
    m = (_sigmoid(_dot(hb, wg_ref[:, :D])) * _dot(ya_ref[...], wpa_ref[...])
         + _sigmoid(_dot(hb, wg_ref[:, D:])) * _dot(yb.astype(BF16), wpb_ref[...]))
    out = _dot(m.astype(BF16), wo_ref[...])
    y_ref[...] = _normalize(alpha * x_ref[...] + gate * out) * lng_ref[...] + lnb_ref[...]


def _sample_layer(x, mod, cstate, state, mats, small, wsrow, bs_rows, alpha, n_step):
    R, D = x.shape
    L = n_step
    NB = R // L
    G = SAMPLE_GROUP
    assert NB % G == 0 and L >= CONV_W - 1
    npair = len(_pairs(L, False))
    params = pltpu.CompilerParams(vmem_limit_bytes=VMEM_LIMIT_BYTES)
    act = jax.ShapeDtypeStruct((R, D), F32)
    act16 = jax.ShapeDtypeStruct((R, D), BF16)
    h16, ya, vrows, conv_new, u, w, qg, kp, qkd, egl = pl.pallas_call(
        functools.partial(_sample_front_kernel, L),
        out_shape=[act16, act16, act, jax.ShapeDtypeStruct(cstate.shape, F32), act, act, act, act,
                   jax.ShapeDtypeStruct((npair, NB, HEAD_DIM), F32),
                   jax.ShapeDtypeStruct((NB, HEAD_DIM), F32)],
        scratch_shapes=[pltpu.VMEM((R, D), F32)] * 3,
        compiler_params=params,
        name="sample_front",
    )(x, mod, cstate, mats["wuvz"], mats["wqkv"], mats["wab"], wsrow, bs_rows,
      small["lnvg"], small["lnvb"], small["convw"], small["alog"], small["dtb"])

    def rows_spec():
        return pl.BlockSpec((L, G, D), lambda i: (0, i, 0))

    state_spec = pl.BlockSpec((G, N_HEAD, HEAD_DIM, HEAD_DIM), lambda i: (i, 0, 0, 0))
    as3 = lambda a: a.reshape(L, NB, D)
    o, state_new = pl.pallas_call(
        functools.partial(_sample_delta_kernel, L),
        grid=(NB // G,),
        in_specs=[rows_spec(), rows_spec(), rows_spec(), rows_spec(),
                  pl.BlockSpec((npair, G, HEAD_DIM), lambda i: (0, i, 0)),
                  pl.BlockSpec((G, HEAD_DIM), lambda i: (i, 0)),
                  state_spec],
        out_specs=[rows_spec(), state_spec],
        out_shape=[jax.ShapeDtypeStruct((L, NB, D), F32), jax.ShapeDtypeStruct(state.shape, F32)],
        compiler_params=pltpu.CompilerParams(
            dimension_semantics=("arbitrary",), vmem_limit_bytes=VMEM_LIMIT_BYTES),
        name="sample_delta",
    )(as3(u), as3(w), as3(qg), as3(kp), qkd, egl, state)
    y = pl.pallas_call(
        functools.partial(_sample_back_kernel, alpha, L),
        out_shape=act,
        compiler_params=params,
        name="sample_back",
    )(x, mod, h16, ya, o.reshape(R, D), mats["wzb"], mats["wg"], mats["wpa"], mats["wpb"], mats["wo"],
      small["onorm"], small["lng"], small["lnb"])
    return y, conv_new, state_new, vrows


def kernel(x_prompt, x_sample, state_conv, state_ssm, c_prompt, c_sample, w_ada, b_ada, w_in, w_s, b_s,
           lnv_g, lnv_b, conv_w, a_log, dt_bias, onorm_g, w_pa, w_pb, w_o, ln_g, ln_b):
    depth = w_in.shape[0]
    alpha = (2 * depth) ** 0.25
    bp = x_prompt.shape[0]
    nb, ls, d = x_sample.shape
    assert ls <= GMLP_CHUNK
    mod = _ada_mod(jnp.concatenate([c_prompt, c_sample], axis=0), w_ada, b_ada)
    xp = x_prompt
    xs = x_sample.transpose(1, 0, 2).reshape(ls * nb, d)
    outs = [[] for _ in range(6)]
    for l in range(depth):
        mats, small = _layer_weights(l, w_in, w_s, b_s, lnv_g, lnv_b, conv_w, a_log, dt_bias, onorm_g,
                                     w_pa, w_pb, w_o, ln_g, ln_b)
        xp, cp, sp, vp = _prompt_layer(xp, mod[l, :bp, None, :], _prompt_operands(mats, small, w_s[l], b_s[l]),
                                       alpha)
        wsrow = jnp.repeat(w_s[l][:, :ls, :ls].transpose(1, 2, 0), HEAD_DIM, axis=-1)
        bs_rows = jnp.repeat(b_s[l][:, :ls].T, HEAD_DIM, axis=-1)
        xs, cs, ss, vs = _sample_layer(xs, mod[l, bp:], state_conv[l].transpose(1, 0, 2), state_ssm[l],
                                       mats, small, wsrow, bs_rows, alpha, ls)
        for lst, val in zip(outs, (cp, sp, vp, cs.transpose(1, 0, 2), ss,
                                   vs.reshape(ls, nb, d).transpose(1, 0, 2))):
            lst.append(val)
    y_sample = xs.reshape(ls, nb, d).transpose(1, 0, 2)
    return (xp, y_sample) + tuple(jnp.stack(o) for o in outs)
```
